```python
import jax, jax.numpy as jnp
from jax import lax
import numpy as np

D_MODEL = 1024
BATCH = 8
SEQ = 8192
DEPTH = 1
DEC_BATCH = 32
DEC_SEQ = 32
PAST_LEN = 1024

CHUNK = 64
MIX_WIDTH = D_MODEL
POOL_WIDTH = MIX_WIDTH // 2
POOL_WINDOWS = (2, 4, 8, 16)
POOL_GROUPS = len(POOL_WINDOWS)
POOL_GROUP_WIDTH = POOL_WIDTH // POOL_GROUPS
POOL_HIST = max(POOL_WINDOWS) - 1
MLSTM_WIDTH = MIX_WIDTH - POOL_WIDTH
MLSTM_HEADS = 4
HEAD_DIM = MLSTM_WIDTH // MLSTM_HEADS
D_FF = -(-8 * D_MODEL // (3 * 256)) * 256
IN_COLS = POOL_WIDTH + 4 * MLSTM_WIDTH + 2 * MLSTM_HEADS
SPLITS = [int(s) for s in np.cumsum([POOL_WIDTH, MLSTM_WIDTH, MLSTM_WIDTH, MLSTM_WIDTH, MLSTM_WIDTH, MLSTM_HEADS])]
ALPHA = (2.0 * DEPTH) ** 0.25
BETA = (8.0 * DEPTH) ** -0.25
LN_EPS = 1e-5

kernel_name = 'hymba_pool_mlstm_stream_step'


def layer_norm(x, g, b):
    xf = x.astype(jnp.float32)
    mu = jnp.mean(xf, axis=-1, keepdims=True)
    var = jnp.mean(jnp.square(xf - mu), axis=-1, keepdims=True)
    y = (xf - mu) * lax.rsqrt(var + LN_EPS) * g.astype(jnp.float32) + b.astype(jnp.float32)
    return y.astype(x.dtype)


def pool_mixer(u, hist, pos0, w_pool, pool_scale):
    T = u.shape[1]
    ext = jnp.concatenate([hist.astype(u.dtype), u], axis=1)
    extf = ext.astype(jnp.float32)
    cs = jnp.pad(jnp.cumsum(extf, axis=1), ((0, 0), (1, 0), (0, 0)))
    end = cs[:, POOL_HIST + 1:POOL_HIST + 1 + T]
    idx = pos0 + jnp.arange(T) + 1
    outs = []
    for g, w in enumerate(POOL_WINDOWS):
        ch = slice(g * POOL_GROUP_WIDTH, (g + 1) * POOL_GROUP_WIDTH)
        start = cs[:, POOL_HIST + 1 - w:POOL_HIST + 1 - w + T, ch]
        cnt = jnp.minimum(idx, w).astype(jnp.float32)[None, :, None]
        outs.append((end[..., ch] - start) / cnt - extf[:, POOL_HIST:, ch])
    d = jnp.stack(outs, axis=2)
    p = jnp.einsum('btgc,gcd->btgd', d, w_pool.astype(jnp.float32)).reshape(u.shape)
    p = p * pool_scale.astype(jnp.float32)
    return p.astype(u.dtype), ext[:, -POOL_HIST:]


def mlstm_chunk(carry, inp):
    C, n, m = carry
    q, k, v, ig, logf = inp
    L = q.shape[2]
    b = jnp.cumsum(logf, axis=-1)
    m_t = b + jnp.maximum(m[..., None], lax.cummax(ig - b, axis=2))
    dec = jnp.exp(b + m[..., None] - m_t)
    causal = jnp.tril(jnp.ones((L, L), dtype=bool))
    log_d = b[..., :, None] - b[..., None, :] + ig[..., None, :] - m_t[..., :, None]
    dmat = jnp.exp(jnp.where(causal, log_d, -jnp.inf))
    s = jnp.einsum('bhtd,bhsd->bhts', q, k) * dmat
    num = dec[..., None] * jnp.einsum('bhtk,bhkv->bhtv', q, C) + jnp.einsum('bhts,bhsv->bhtv', s, v)
    den = dec * jnp.einsum('bhtk,bhk->bht', q, n) + jnp.sum(s, axis=-1)
    h = num / jnp.maximum(jnp.abs(den), jnp.exp(-m_t))[..., None]
    m_new = m_t[..., -1]
    w_state = jnp.exp(b[..., -1] + m - m_new)
    w_row = jnp.exp(b[..., -1:] - b + ig - m_new[..., None])
    C_new = w_state[..., None, None] * C + jnp.einsum('bhs,bhsk,bhsv->bhkv', w_row, k, v)
    n_new = w_state[..., None] * n + jnp.einsum('bhs,bhsk->bhk', w_row, k)
    return (C_new, n_new, m_new), h


def mlstm_mixer(q, k, v, o, ig, fg, C0, n0, m0, norm_g):
    B, T, _ = q.shape
    L = min(T, CHUNK)
    nc = T // L
    f32 = jnp.float32

    def heads(a):
        return a.reshape(B, nc, L, MLSTM_HEADS, HEAD_DIM).transpose(1, 0, 3, 2, 4).astype(f32)

    def gates(a):
        return a.reshape(B, nc, L, MLSTM_HEADS).transpose(1, 0, 3, 2).astype(f32)

    xs = (heads(q), heads(k) * (HEAD_DIM ** -0.5), heads(v), gates(ig), jax.nn.log_sigmoid(gates(fg)))
    carry, hs = lax.scan(mlstm_chunk, (C0.astype(f32), n0.astype(f32), m0.astype(f32)), xs)
    h = hs.transpose(1, 0, 3, 2, 4).reshape(B, T, MLSTM_HEADS, HEAD_DIM)
    mu = jnp.mean(h, axis=-1, keepdims=True)
    var = jnp.mean(jnp.square(h - mu), axis=-1, keepdims=True)
    h = ((h - mu) * lax.rsqrt(var + LN_EPS)).reshape(B, T, MLSTM_WIDTH)
    h = h * norm_g.astype(f32) * jax.nn.sigmoid(o.astype(f32))
    return h.astype(q.dtype), carry


def trunk_layer(x, pool_hist, C0, n0, m0, pos0, w_in, b_in, w_pool, pool_scale, mlstm_norm_g, w_out,
                ln1_g, ln1_b, w_gate, w_up, w_down, ln2_g, ln2_b):
    proj = x @ w_in + b_in
    u, q, k, v, o, ig, fg = jnp.split(proj, SPLITS, axis=-1)
    p_out, pool_new = pool_mixer(u, pool_hist, pos0, w_pool, pool_scale)
    m_out, (C, n, m) = mlstm_mixer(q, k, v, o, ig, fg, C0, n0, m0, mlstm_norm_g)
    mix = jnp.concatenate([p_out, m_out], axis=-1) @ w_out
    h = layer_norm(ALPHA * x + mix, ln1_g, ln1_b)
    ff = (jax.nn.silu(h @ w_gate) * (h @ w_up)) @ w_down
    y = layer_norm(ALPHA * h + ff, ln2_g, ln2_b)
    dt = x.dtype
    return y, (pool_new.astype(dt), C.astype(dt), n.astype(dt), m.astype(dt))


def setup_inputs(seed: int = 0) -> dict:
    key = jax.random.key(seed)
    ks = jax.random.split(key, 24)
    f32 = jnp.float32

    def nrm(k, shape, s):
        return jax.random.normal(k, shape, f32) * s

    x_prompt = nrm(ks[0], (BATCH, SEQ, D_MODEL), 1.0)
    x_sample = nrm(ks[1], (DEC_BATCH, DEC_SEQ, D_MODEL), 1.0)
    state_pool = nrm(ks[2], (DEPTH, DEC_BATCH, POOL_HIST, POOL_WIDTH), 1.0)
    state_mlstm_C = nrm(ks[3], (DEPTH, DEC_BATCH, MLSTM_HEADS, HEAD_DIM, HEAD_DIM), 0.3)
    state_mlstm_n = nrm(ks[4], (DEPTH, DEC_BATCH, MLSTM_HEADS, HEAD_DIM), 0.3)
    state_mlstm_m = nrm(ks[5], (DEPTH, DEC_BATCH, MLSTM_HEADS), 1.0)
    ln_in_g = 1.0 + nrm(ks[6], (D_MODEL,), 0.02)
    ln_in_b = nrm(ks[7], (D_MODEL,), 0.01)
    col_scale = jnp.ones((IN_COLS,), f32).at[SPLITS[2]:SPLITS[3]].set(BETA)
    w_in = nrm(ks[8], (DEPTH, D_MODEL, IN_COLS), D_MODEL ** -0.5) * col_scale
    f_bias = jnp.concatenate([jnp.zeros((IN_COLS - MLSTM_HEADS,), f32), jnp.linspace(3.0, 6.0, MLSTM_HEADS)])
    b_in = nrm(ks[9], (DEPTH, IN_COLS), 0.01) + f_bias
    w_pool = nrm(ks[10], (DEPTH, POOL_GROUPS, POOL_GROUP_WIDTH, POOL_GROUP_WIDTH), POOL_GROUP_WIDTH ** -0.5)
    pool_scale = 1.0 + nrm(ks[11], (DEPTH, POOL_WIDTH), 0.02)
    mlstm_norm_g = 1.0 + nrm(ks[12], (DEPTH, MLSTM_WIDTH), 0.02)
    w_out = nrm(ks[13], (DEPTH, MIX_WIDTH, D_MODEL), BETA * MIX_WIDTH ** -0.5)
    ln1_g = 1.0 + nrm(ks[14], (DEPTH, D_MODEL), 0.02)
    ln1_b = nrm(ks[15], (DEPTH, D_MODEL), 0.01)
    w_gate = nrm(ks[16], (DEPTH, D_MODEL, D_FF), D_MODEL ** -0.5)
    w_up = nrm(ks[17], (DEPTH, D_MODEL, D_FF), D_MODEL ** -0.5)
    w_down = nrm(ks[18], (DEPTH, D_FF, D_MODEL), BETA * D_FF ** -0.5)
    ln2_g = 1.0 + nrm(ks[19], (DEPTH, D_MODEL), 0.02)
    ln2_b = nrm(ks[20], (DEPTH, D_MODEL), 0.01)
    return {'x_prompt': x_prompt, 'x_sample': x_sample, 'state_pool': state_pool,
            'state_mlstm_C': state_mlstm_C, 'state_mlstm_n': state_mlstm_n, 'state_mlstm_m': state_mlstm_m,
            'ln_in_g': ln_in_g, 'ln_in_b': ln_in_b, 'w_in': w_in, 'b_in': b_in, 'w_pool': w_pool,
            'pool_scale': pool_scale, 'mlstm_norm_g': mlstm_norm_g, 'w_out': w_out,
            'ln1_g': ln1_g, 'ln1_b': ln1_b, 'w_gate': w_gate, 'w_up': w_up, 'w_down': w_down,
            'ln2_g': ln2_g, 'ln2_b': ln2_b}


def reference(x_prompt, x_sample, state_pool, state_mlstm_C, state_mlstm_n, state_mlstm_m,
              ln_in_g, ln_in_b, w_in, b_in, w_pool, pool_scale, mlstm_norm_g, w_out,
              ln1_g, ln1_b, w_gate, w_up, w_down, ln2_g, ln2_b):
    dt = x_prompt.dtype
    bp = x_prompt.shape[0]
    hp = layer_norm(x_prompt, ln_in_g, ln_in_b)
    hs = layer_norm(x_sample, ln_in_g, ln_in_b)
    zero_pool = jnp.zeros((bp, POOL_HIST, POOL_WIDTH), dt)
    zero_C = jnp.zeros((bp, MLSTM_HEADS, HEAD_DIM, HEAD_DIM), dt)
    zero_n = jnp.zeros((bp, MLSTM_HEADS, HEAD_DIM), dt)
    zero_m = jnp.zeros((bp, MLSTM_HEADS), dt)
    acc_p = [[], [], [], []]
    acc_s = [[], [], [], []]
    for l in range(DEPTH):
        params = (w_in[l], b_in[l], w_pool[l], pool_scale[l], mlstm_norm_g[l], w_out[l],
                  ln1_g[l], ln1_b[l], w_gate[l], w_up[l], w_down[l], ln2_g[l], ln2_b[l])
        hp, st_p = trunk_layer(hp, zero_pool, zero_C, zero_n, zero_m, 0, *params)
        hs, st_s = trunk_layer(hs, state_pool[l], state_mlstm_C[l], state_mlstm_n[l], state_mlstm_m[l],
                               PAST_LEN, *params)
        for a, s in zip(acc_p, st_p):
            a.append(s)
        for a, s in zip(acc_s, st_s):
            a.append(s)
    pool_p, C_p, n_p, m_p = [jnp.stack(a) for a in acc_p]
    pool_s, C_s, n_s, m_s = [jnp.stack(a) for a in acc_s]
    return (hp, hs, pool_p, C_p, n_p, m_p, pool_s, C_s, n_s, m_s)
```

```python
import functools

import jax
import jax.numpy as jnp
from jax import lax
from jax.experimental import pallas as pl
from jax.experimental.pallas import tpu as pltpu

D_MODEL = 1024
POOL_WIDTH = 512
POOL_WINDOWS = (2, 4, 8, 16)
POOL_GROUP_WIDTH = 128
POOL_HIST = 15
HIST_ROWS = 16
MLSTM_WIDTH = 512
MLSTM_HEADS = 4
HEAD_DIM = 128
D_FF = 2816
LN_EPS = 1e-5

LANES = 128
SUBLANES = 8
VMEM_LIMIT_BYTES = 56 * 1024 * 1024

_NT = (((1,), (1,)), ((), ()))


def _layer_norm(x, g, b):
    mu = jnp.mean(x, axis=-1, keepdims=True)
    xc = x - mu
    var = jnp.mean(xc * xc, axis=-1, keepdims=True)
    return xc * lax.rsqrt(var + LN_EPS) * g + b


def _log_sigmoid(x):
    return -(jnp.maximum(-x, 0.0) + jnp.log1p(jnp.exp(-jnp.abs(x))))


def _dot(a, b):
    return jnp.dot(a, b, preferred_element_type=jnp.float32)


def _mixer_kernel(x_ref, hist_ref, c0_ref, m0_ref, lng_ref, lnb_ref, w_in_ref, b_in_ref, wkT_ref, bkT_ref,
                  wgT_ref, bgT_ref, w_pool_ref, pscale_ref, ng_ref, w_out_ref, ln1g_ref, ln1b_ref,
                  h_out_ref, pool_out_ref, c_out_ref, m_out_ref,
                  hp_ref, ext_ref, q_ref, kT_ref, v_ref, o_ref, mix_ref, c_ref, pt_ref,
                  *, tt, mp, pos0, in_ln, alpha):
    t = pl.program_id(1)
    f32 = jnp.float32
    bf16 = jnp.bfloat16

    @pl.when(t == 0)
    def _init():
        ext_ref[0:HIST_ROWS, :] = hist_ref[0]
        c_out_ref[0] = c0_ref[0]
        m_out_ref[0] = m0_ref[0]
        for h in range(MLSTM_HEADS):
            v_ref[:, h * 256 + HEAD_DIM:(h + 1) * 256] = jnp.ones((mp, HEAD_DIM), bf16)

    x = x_ref[0]
    hp = _layer_norm(x, lng_ref[...], lnb_ref[...]) if in_ln else x
    hp_ref[...] = hp
    hp_bf = hp.astype(bf16)
    if mp > tt:
        hp_bf = jnp.concatenate([hp_bf, jnp.zeros((mp - tt, D_MODEL), bf16)], axis=0)

    u = _dot(hp_bf, w_in_ref[:, 0:512]) + b_in_ref[:, 0:512]
    ext_ref[HIST_ROWS:HIST_ROWS + tt, :] = u[:tt]
    q = _dot(hp_bf, w_in_ref[:, 512:1024]) + b_in_ref[:, 512:1024]
    q_ref[...] = q.astype(bf16)
    v = _dot(hp_bf, w_in_ref[:, 1024:1536]) + b_in_ref[:, 1024:1536]
    for h in range(MLSTM_HEADS):
        v_ref[:, h * 256:h * 256 + HEAD_DIM] = v[:, h * HEAD_DIM:(h + 1) * HEAD_DIM].astype(bf16)
    o = _dot(hp_bf, w_in_ref[:, 1536:2048]) + b_in_ref[:, 1536:2048]
    o_ref[...] = o[:tt]
    kT = lax.dot_general(wkT_ref[...], hp_bf, _NT, preferred_element_type=f32) + bkT_ref[...]
    kT_ref[...] = kT * (HEAD_DIM ** -0.5)
    gT = lax.dot_general(wgT_ref[...], hp_bf, _NT, preferred_element_type=f32) + bgT_ref[...]

    lane = lax.broadcasted_iota(jnp.int32, (SUBLANES, LANES), 1)
    sub = lax.broadcasted_iota(jnp.int32, (SUBLANES, LANES), 0)
    n_chunks = mp // LANES
    for j in range(n_chunks):
        g8 = gT[:, j * LANES:(j + 1) * LANES]
        b8 = _log_sigmoid(g8)
        k = 1
        while k < LANES:
            b8 = b8 + jnp.where(lane >= k, pltpu.roll(b8, k, 1), 0.0)
            k *= 2
        c8 = g8 - pltpu.roll(b8, MLSTM_HEADS, 0)
        cm8 = c8
        k = 1
        while k < LANES:
            cm8 = jnp.maximum(cm8, jnp.where(lane >= k, pltpu.roll(cm8, k, 1), -jnp.inf))
            k *= 2
        c_ref[:, j * LANES:(j + 1) * LANES] = c8
        pt_ref[j * LANES:(j + 1) * LANES, :] = jnp.where(sub < MLSTM_HEADS, cm8, b8).T

    row = lax.broadcasted_iota(jnp.int32, (tt, 1), 0)
    n_avail = pos0 + t * tt + row + 1
    for g, w in enumerate(POOL_WINDOWS):
        gs = slice(g * POOL_GROUP_WIDTH, (g + 1) * POOL_GROUP_WIDTH)
        tok = ext_ref[HIST_ROWS:HIST_ROWS + tt, gs]
        acc = tok
        for s in range(1, w):
            acc = acc + ext_ref[HIST_ROWS - s:HIST_ROWS - s + tt, gs]
        inv_cnt = 1.0 / jnp.minimum(n_avail, w).astype(f32)
        d = acc * inv_cnt - tok
        p = _dot(d.astype(bf16), w_pool_ref[g]) * pscale_ref[:, gs]
        mix_ref[:, gs] = p.astype(bf16)
    pool_out_ref[0] = ext_ref[tt:tt + HIST_ROWS, :]
    ext_ref[0:HIST_ROWS, :] = ext_ref[tt:tt + HIST_ROWS, :]

    tril = (lax.broadcasted_iota(jnp.int32, (LANES, LANES), 0)
            >= lax.broadcasted_iota(jnp.int32, (LANES, LANES), 1))
    lane_row = lax.broadcasted_iota(jnp.int32, (1, LANES), 1)
    for j in range(n_chunks):
        rows = slice(j * LANES, (j + 1) * LANES)
        valid = min(LANES, tt - j * LANES)
        for h in range(MLSTM_HEADS):
            hs = slice(h * HEAD_DIM, (h + 1) * HEAD_DIM)
            qh = q_ref[rows, hs]
            kTh = kT_ref[hs, rows]
            vaug = v_ref[rows, h * 256:(h + 1) * 256]
            c_row = c_ref[h:h + 1, rows]
            cmax = jnp.broadcast_to(pt_ref[rows, h:h + 1], (LANES, LANES))
            bcum = jnp.broadcast_to(pt_ref[rows, MLSTM_HEADS + h:MLSTM_HEADS + h + 1], (LANES, LANES))
            m_row = m_out_ref[0, h:h + 1, :]
            c_aug = c_out_ref[0, h]

            gmax = jnp.maximum(m_row, cmax)
            dmat = jnp.where(tril, jnp.exp(c_row - gmax), 0.0)
            s = _dot(qh, kTh.astype(bf16)) * dmat
            sv = _dot(s.astype(bf16), vaug)
            qc = _dot(qh, c_aug.astype(bf16))
            dec = jnp.exp(m_row - gmax)
            num = dec * qc[:, :HEAD_DIM] + sv[:, :HEAD_DIM]
            den = dec * qc[:, HEAD_DIM:] + sv[:, HEAD_DIM:]
            hh = num / jnp.maximum(jnp.abs(den), jnp.exp(-(bcum + gmax)))

            mu = jnp.mean(hh, axis=-1, keepdims=True)
            hc = hh - mu
            var = jnp.mean(hc * hc, axis=-1, keepdims=True)
            hn = hc * lax.rsqrt(var + LN_EPS)
            out = hn[:valid] * ng_ref[:, hs] * jax.nn.sigmoid(o_ref[j * LANES:j * LANES + valid, hs])
            mix_ref[j * LANES:j * LANES + valid, POOL_WIDTH + h * HEAD_DIM:POOL_WIDTH + (h + 1) * HEAD_DIM] = (
                out.astype(bf16))

            g_last = gmax[valid - 1:valid, :]
            m_new = bcum[valid - 1:valid, :] + g_last
            w_state = jnp.exp(m_row - g_last)
            w_row = jnp.exp(c_row - g_last)
            if valid < LANES:
                w_row = jnp.where(lane_row < valid, w_row, 0.0)
            kv = _dot((kTh * w_row).astype(bf16), vaug)
            c_out_ref[0, h] = jnp.concatenate([w_state, w_state], axis=1) * c_aug + kv
            m_out_ref[0, h:h + 1, :] = m_new

    mixo = _dot(mix_ref[...], w_out_ref[...])
    h_out_ref[0] = _layer_norm(alpha * hp_ref[...] + mixo, ln1g_ref[...], ln1b_ref[...])


def _const_spec(shape):
    zeros = (0,) * len(shape)
    return pl.BlockSpec(shape, lambda *_: zeros, pipeline_mode=pl.Buffered(1))


def _mixer_tiles(seq_len):
    tt = min(seq_len, 512)
    assert seq_len % tt == 0 and tt % HIST_ROWS == 0
    mp = -(-tt // LANES) * LANES
    assert mp == tt or seq_len == tt
    return tt, mp


def _mixer(x, hist, c0, m0, wts, *, pos0, in_ln, alpha):
    batch, seq_len, _ = x.shape
    tt, mp = _mixer_tiles(seq_len)
    n_t = seq_len // tt
    f32, bf16 = jnp.float32, jnp.bfloat16
    kern = functools.partial(_mixer_kernel, tt=tt, mp=mp, pos0=pos0, in_ln=in_ln, alpha=alpha)
    per_b = lambda b, t: (b, 0, 0)
    in_specs = [
        pl.BlockSpec((1, tt, D_MODEL), lambda b, t: (b, t, 0)),
        pl.BlockSpec((1, HIST_ROWS, POOL_WIDTH), per_b),
        pl.BlockSpec((1, MLSTM_HEADS, HEAD_DIM, 2 * HEAD_DIM), lambda b, t: (b, 0, 0, 0)),
        pl.BlockSpec((1, SUBLANES, LANES), per_b),
    ] + [_const_spec(w.shape) for w in wts]
    out_shape = (
        jax.ShapeDtypeStruct((batch, seq_len, D_MODEL), f32),
        jax.ShapeDtypeStruct((batch, HIST_ROWS, POOL_WIDTH), f32),
        jax.ShapeDtypeStruct((batch, MLSTM_HEADS, HEAD_DIM, 2 * HEAD_DIM), f32),
        jax.ShapeDtypeStruct((batch, SUBLANES, LANES), f32),
    )
    out_specs = (
        pl.BlockSpec((1, tt, D_MODEL), lambda b, t: (b, t, 0)),
        pl.BlockSpec((1, HIST_ROWS, POOL_WIDTH), per_b),
        pl.BlockSpec((1, MLSTM_HEADS, HEAD_DIM, 2 * HEAD_DIM), lambda b, t: (b, 0, 0, 0)),
        pl.BlockSpec((1, SUBLANES, LANES), per_b),
    )
    scratch = [
        pltpu.VMEM((tt, D_MODEL), f32),
        pltpu.VMEM((HIST_ROWS + tt, POOL_WIDTH), f32),
        pltpu.VMEM((mp, MLSTM_WIDTH), bf16),
        pltpu.VMEM((MLSTM_WIDTH, mp), f32),
        pltpu.VMEM((mp, 2 * MLSTM_WIDTH), bf16),
        pltpu.VMEM((tt, MLSTM_WIDTH), f32),
        pltpu.VMEM((tt, D_MODEL), bf16),
        pltpu.VMEM((SUBLANES, mp), f32),
        pltpu.VMEM((mp, SUBLANES), f32),
    ]
    return pl.pallas_call(
        kern,
        grid=(batch, n_t),
        in_specs=in_specs,
        out_specs=out_specs,
        out_shape=out_shape,
        scratch_shapes=scratch,
        compiler_params=pltpu.CompilerParams(
            dimension_semantics=("arbitrary", "arbitrary"), vmem_limit_bytes=VMEM_LIMIT_BYTES),
        name="mixer",
    )(x, hist, c0, m0, *wts)


FF_CHUNK = 256


def _ffn_kernel(h_ref, wg_ref, wu_ref, wd_ref, g_ref, b_ref, y_ref, acc_ref, *, alpha):
    h = h_ref[...]
    hb = h.astype(jnp.bfloat16)
    for ci in range(D_FF // FF_CHUNK):
        cs = slice(ci * FF_CHUNK, (ci + 1) * FF_CHUNK)
        gate = _dot(hb, wg_ref[:, cs])
        up = _dot(hb, wu_ref[:, cs])
        part = _dot((jax.nn.silu(gate) * up).astype(jnp.bfloat16), wd_ref[cs, :])
        if ci == 0:
            acc_ref[...] = part
        else:
            acc_ref[...] += part
    y_ref[...] = _layer_norm(alpha * h + acc_ref[...], g_ref[...], b_ref[...])


def _ffn(h, wg, wu, wd, g, b, *, alpha):
    n = h.shape[0]
    tm = min(n, 512)
    assert n % tm == 0
    row_spec = pl.BlockSpec((tm, D_MODEL), lambda i: (i, 0))
    return pl.pallas_call(
        functools.partial(_ffn_kernel, alpha=alpha),
        grid=(n // tm,),
        in_specs=[row_spec] + [_const_spec(w.shape) for w in (wg, wu, wd, g, b)],
        out_specs=row_spec,
        out_shape=jax.ShapeDtypeStruct((n, D_MODEL), jnp.float32),
        scratch_shapes=[pltpu.VMEM((tm, D_MODEL), jnp.float32)],
        compiler_params=pltpu.CompilerParams(
            dimension_semantics=("arbitrary",), vmem_limit_bytes=VMEM_LIMIT_BYTES),
        name="ffn",
    )(h, wg, wu, wd, g, b)


def _mixer_weights(ln_in_g, ln_in_b, w_in, b_in, w_pool, pool_scale, mlstm_norm_g, w_out, ln1_g, ln1_b):
    f32, bf16 = jnp.float32, jnp.bfloat16
    k0, k1, g0 = 2 * 512, 3 * 512, 5 * 512
    w_uqvo = jnp.concatenate([w_in[:, :k0], w_in[:, k1:g0]], axis=1).astype(bf16)
    b_uqvo = jnp.concatenate([b_in[:k0], b_in[k1:g0]]).reshape(1, -1).astype(f32)
    return (
        ln_in_g.reshape(1, -1), ln_in_b.reshape(1, -1), w_uqvo, b_uqvo,
        w_in[:, k0:k1].T.astype(bf16), b_in[k0:k1].reshape(-1, 1).astype(f32),
        w_in[:, g0:].T.astype(bf16), b_in[g0:].reshape(-1, 1).astype(f32),
        w_pool.astype(bf16), pool_scale.reshape(1, -1), mlstm_norm_g.reshape(1, -1), w_out.astype(bf16),
        ln1_g.reshape(1, -1), ln1_b.reshape(1, -1),
    )


def _pack_state(pool, c, n, m):
    batch = pool.shape[0]
    hist = jnp.pad(pool, ((0, 0), (HIST_ROWS - POOL_HIST, 0), (0, 0)))
    n_rep = jnp.broadcast_to(n[..., None], (batch, MLSTM_HEADS, HEAD_DIM, HEAD_DIM))
    c_aug = jnp.concatenate([c, n_rep], axis=-1)
    m_rep = jnp.broadcast_to(m[..., None], (batch, MLSTM_HEADS, LANES))
    m_rep = jnp.pad(m_rep, ((0, 0), (0, SUBLANES - MLSTM_HEADS), (0, 0)))
    return hist, c_aug, m_rep


def _unpack_state(pool_out, c_out, m_out):
    return (pool_out[:, HIST_ROWS - POOL_HIST:], c_out[..., :HEAD_DIM], c_out[..., HEAD_DIM],
            m_out[:, :MLSTM_HEADS, 0])


def kernel(x_prompt, x_sample, state_pool, state_mlstm_C, state_mlstm_n, state_mlstm_m, ln_in_g, ln_in_b, w_in, b_in, w_pool, pool_scale, mlstm_norm_g, w_out, ln1_g, ln1_b, w_gate, w_up, w_down, ln2_g, ln2_b):
    depth = w_in.shape[0]
    past_len = 1024
    alpha = (2.0 * depth) ** 0.25
    f32, bf16 = jnp.float32, jnp.bfloat16
    bp = x_prompt.shape[0]
    zero_state = _pack_state(jnp.zeros((bp, POOL_HIST, POOL_WIDTH), f32),
                             jnp.zeros((bp, MLSTM_HEADS, HEAD_DIM, HEAD_DIM), f32),
                             jnp.zeros((bp, MLSTM_HEADS, HEAD_DIM), f32), jnp.zeros((bp, MLSTM_HEADS), f32))
    hp, hs = x_prompt, x_sample
    acc_p, acc_s = [], []
    for l in range(depth):
        wts = _mixer_weights(ln_in_g, ln_in_b, w_in[l], b_in[l], w_pool[l], pool_scale[l], mlstm_norm_g[l],
                             w_out[l], ln1_g[l], ln1_b[l])
        ffn_w = (w_gate[l].astype(bf16), w_up[l].astype(bf16), w_down[l].astype(bf16),
                 ln2_g[l].reshape(1, -1), ln2_b[l].reshape(1, -1))
        sample_state = _pack_state(state_pool[l], state_mlstm_C[l], state_mlstm_n[l], state_mlstm_m[l])
        outs = []
        for x, state, pos0 in ((hp, zero_state, 0), (hs, sample_state, past_len)):
            h1, pool_o, c_o, m_o = _mixer(x, *state, wts, pos0=pos0, in_ln=(l == 0), alpha=alpha)
            y = _ffn(h1.reshape(-1, D_MODEL), *ffn_w, alpha=alpha).reshape(x.shape)
            outs.append((y, _unpack_state(pool_o, c_o, m_o)))
        (hp, st_p), (hs, st_s) = outs
        acc_p.append(st_p)
        acc_s.append(st_s)
    pool_p, c_p, n_p, m_p = [jnp.stack(a) for a in zip(*acc_p)]
    pool_s, c_s, n_s, m_s = [jnp.stack(a) for a in zip(*acc_s)]
    return (hp, hs, pool_p, c_p, n_p, m_p, pool_s, c_s, n_s, m_s)
```

```python
import functools

import jax
import jax.numpy as jnp
from jax import lax
from jax.experimental import pallas as pl
from jax.experimental.pallas import tpu as pltpu

D_MODEL = 1024
POOL_WIDTH = 512
POOL_WINDOWS = (2, 4, 8, 16)
POOL_GROUP_WIDTH = 128
POOL_HIST = 15
HIST_ROWS = 16
MLSTM_WIDTH = 512
MLSTM_HEADS = 4
HEAD_DIM = 128
D_FF = 2816
LN_EPS = 1e-5

LANES = 128
SUBLANES = 8
VMEM_LIMIT_BYTES = 56 * 1024 * 1024

_NT = (((1,), (1,)), ((), ()))


def _layer_norm(x, g, b):
    mu = jnp.mean(x, axis=-1, keepdims=True)
    xc = x - mu
    var = jnp.mean(xc * xc, axis=-1, keepdims=True)
    return xc * lax.rsqrt(var + LN_EPS) * g + b


def _log_sigmoid(x):
    return -(jnp.maximum(-x, 0.0) + jnp.log1p(jnp.exp(-jnp.abs(x))))


def _dot(a, b):
    return jnp.dot(a, b, preferred_element_type=jnp.float32)


def _mixer_kernel(x_ref, hist_ref, c0_ref, m0_ref, lng_ref, lnb_ref, w_in_ref, b_in_ref, wkT_ref, bkT_ref,
                  wgT_ref, bgT_ref, w_pool_ref, pscale_ref, ng_ref, w_out_ref, ln1g_ref, ln1b_ref,
                  h_out_ref, pool_out_ref, c_out_ref, m_out_ref,
                  hp_ref, ext_ref, q_ref, kT_ref, v_ref, o_ref, mix_ref, c_ref, pt_ref,
                  *, tt, mp, pos0, in_ln, alpha):
    t = pl.program_id(1)
    f32 = jnp.float32
    bf16 = jnp.bfloat16

    @pl.when(t == 0)
    def _init():
        ext_ref[0:HIST_ROWS, :] = hist_ref[0]
        c_out_ref[0] = c0_ref[0]
        m_out_ref[0] = m0_ref[0]
        for h in range(MLSTM_HEADS):
            v_ref[:, h * 256 + HEAD_DIM:(h + 1) * 256] = jnp.ones((mp, HEAD_DIM), bf16)

    x = x_ref[0]
    hp = _layer_norm(x, lng_ref[...], lnb_ref[...]) if in_ln else x
    hp_ref[...] = hp
    hp_bf = hp.astype(bf16)
    if mp > tt:
        hp_bf = jnp.concatenate([hp_bf, jnp.zeros((mp - tt, D_MODEL), bf16)], axis=0)

    gT = lax.dot_general(wgT_ref[...], hp_bf, _NT, preferred_element_type=f32) + bgT_ref[...]

    lane = lax.broadcasted_iota(jnp.int32, (SUBLANES, LANES), 1)
    sub = lax.broadcasted_iota(jnp.int32, (SUBLANES, LANES), 0)
    n_chunks = mp // LANES
    for j in range(n_chunks):
        g8 = gT[:, j * LANES:(j + 1) * LANES]
        b8 = _log_sigmoid(g8)
        k = 1
        while k < LANES:
            b8 = b8 + jnp.where(lane >= k, pltpu.roll(b8, k, 1), 0.0)
            k *= 2
        c8 = g8 - pltpu.roll(b8, MLSTM_HEADS, 0)
        cm8 = c8
        k = 1
        while k < LANES:
            cm8 = jnp.maximum(cm8, jnp.where(lane >= k, pltpu.roll(cm8, k, 1), -jnp.inf))
            k *= 2
        c_ref[:, j * LANES:(j + 1) * LANES] = c8
        pt_ref[j * LANES:(j + 1) * LANES, :] = jnp.where(sub < MLSTM_HEADS, cm8, b8).T

    u =_dot(hp_bf, w_in_ref[:, 0:512]) + b_in_ref[:, 0:512]
    ext_ref[HIST_ROWS:HIST_ROWS + tt, :] = u[:tt]
    q = _dot(hp_bf, w_in_ref[:, 512:1024]) + b_in_ref[:, 512:1024]
    q_ref[...] = q.astype(bf16)
    v = _dot(hp_bf, w_in_ref[:, 1024:1536]) + b_in_ref[:, 1024:1536]
    for h in range(MLSTM_HEADS):
        v_ref[:, h * 256:h * 256 + HEAD_DIM] = v[:, h * HEAD_DIM:(h + 1) * HEAD_DIM].astype(bf16)
    o = _dot(hp_bf, w_in_ref[:, 1536:2048]) + b_in_ref[:, 1536:2048]
    o_ref[...] = o[:tt]
    kT = lax.dot_general(wkT_ref[...], hp_bf, _NT, preferred_element_type=f32) + bkT_ref[...]
    kT_ref[...] = kT * (HEAD_DIM ** -0.5)

    row = lax.broadcasted_iota(jnp.int32, (tt, 1), 0)
    n_avail = pos0 + t * tt + row + 1
    for g, w in enumerate(POOL_WINDOWS):
        gs = slice(g * POOL_GROUP_WIDTH, (g + 1) * POOL_GROUP_WIDTH)
        tok = ext_ref[HIST_ROWS:HIST_ROWS + tt, gs]
        acc = tok
        for s in range(1, w):
            acc = acc + ext_ref[HIST_ROWS - s:HIST_ROWS - s + tt, gs]
        inv_cnt = 1.0 / jnp.minimum(n_avail, w).astype(f32)
        d = acc * inv_cnt - tok
        p = _dot(d.astype(bf16), w_pool_ref[g]) * pscale_ref[:, gs]
        mix_ref[:, gs] = p.astype(bf16)
    pool_out_ref[0] = ext_ref[tt:tt + HIST_ROWS, :]
    ext_ref[0:HIST_ROWS, :] = ext_ref[tt:tt + HIST_ROWS, :]

    tril = (lax.broadcasted_iota(jnp.int32, (LANES, LANES), 0)
            >= lax.broadcasted_iota(jnp.int32, (LANES, LANES), 1))
    lane_row = lax.broadcasted_iota(jnp.int32, (1, LANES), 1)
    m_state = [m_out_ref[0, h:h + 1, :] for h in range(MLSTM_HEADS)]
    c_state = [c_out_ref[0, h] for h in range(MLSTM_HEADS)]
    for j in range(n_chunks):
        rows = slice(j * LANES, (j + 1) * LANES)
        valid = min(LANES, tt - j * LANES)
        for h in range(MLSTM_HEADS):
            hs = slice(h * HEAD_DIM, (h + 1) * HEAD_DIM)
            qh = q_ref[rows, hs]
            kTh = kT_ref[hs, rows]
            vaug = v_ref[rows, h * 256:(h + 1) * 256]
            c_row = c_ref[h:h + 1, rows]
            cmax = jnp.broadcast_to(pt_ref[rows, h:h + 1], (LANES, LANES))
            bcum = jnp.broadcast_to(pt_ref[rows, MLSTM_HEADS + h:MLSTM_HEADS + h + 1], (LANES, LANES))
            m_row = m_state[h]
            c_aug = c_state[h]

            gmax = jnp.maximum(m_row, cmax)
            dmat = jnp.where(tril, jnp.exp(c_row - gmax), 0.0)
            s = _dot(qh, kTh.astype(bf16)) * dmat
            sv = _dot(s.astype(bf16), vaug)
            qc = _dot(qh, c_aug.astype(bf16))
            dec = jnp.exp(m_row - gmax)
            num = dec * qc[:, :HEAD_DIM] + sv[:, :HEAD_DIM]
            den = dec * qc[:, HEAD_DIM:] + sv[:, HEAD_DIM:]
            hh = num / jnp.maximum(jnp.abs(den), jnp.exp(-(bcum + gmax)))

            mu = jnp.mean(hh, axis=-1, keepdims=True)
            hc = hh - mu
            var = jnp.mean(hc * hc, axis=-1, keepdims=True)
            hn = hc * lax.rsqrt(var + LN_EPS)
            out = hn[:valid] * ng_ref[:, hs] * jax.nn.sigmoid(o_ref[j * LANES:j * LANES + valid, hs])
            mix_ref[j * LANES:j * LANES + valid, POOL_WIDTH + h * HEAD_DIM:POOL_WIDTH + (h + 1) * HEAD_DIM] = (
                out.astype(bf16))

            g_last = gmax[valid - 1:valid, :]
            m_new = bcum[valid - 1:valid, :] + g_last
            w_state = jnp.exp(m_row - g_last)
            w_row = jnp.exp(c_row - g_last)
            if valid < LANES:
                w_row = jnp.where(lane_row < valid, w_row, 0.0)
            kv = _dot((kTh * w_row).astype(bf16), vaug)
            c_state[h] = jnp.concatenate([w_state, w_state], axis=1) * c_aug + kv
            m_state[h] = m_new
    for h in range(MLSTM_HEADS):
        c_out_ref[0, h] = c_state[h]
        m_out_ref[0, h:h + 1, :] = m_state[h]

    mixo = _dot(mix_ref[...], w_out_ref[...])
    h_out_ref[0] = _layer_norm(alpha * hp_ref[...] + mixo, ln1g_ref[...], ln1b_ref[...])


def _const_spec(shape):
    zeros = (0,) * len(shape)
    return pl.BlockSpec(shape, lambda *_: zeros, pipeline_mode=pl.Buffered(1))


def _mixer_tiles(seq_len):
    tt = min(seq_len, 512)
    assert seq_len % tt == 0 and tt % HIST_ROWS == 0
    mp = -(-tt // LANES) * LANES
    assert mp == tt or seq_len == tt
    return tt, mp


def _mixer(x, hist, c0, m0, wts, *, pos0, in_ln, alpha):
    batch, seq_len, _ = x.shape
    tt, mp = _mixer_tiles(seq_len)
    n_t = seq_len // tt
    f32, bf16 = jnp.float32, jnp.bfloat16
    kern = functools.partial(_mixer_kernel, tt=tt, mp=mp, pos0=pos0, in_ln=in_ln, alpha=alpha)
    per_b = lambda b, t: (b, 0, 0)
    in_specs = [
        pl.BlockSpec((1, tt, D_MODEL), lambda b, t: (b, t, 0)),
        pl.BlockSpec((1, HIST_ROWS, POOL_WIDTH), per_b),
        pl.BlockSpec((1, MLSTM_HEADS, HEAD_DIM, 2 * HEAD_DIM), lambda b, t: (b, 0, 0, 0)),
        pl.BlockSpec((1, SUBLANES, LANES), per_b),
    ] + [_const_spec(w.shape) for w in wts]
    out_shape = (
        jax.ShapeDtypeStruct((batch, seq_len, D_MODEL), f32),
        jax.ShapeDtypeStruct((batch, HIST_ROWS, POOL_WIDTH), f32),
        jax.ShapeDtypeStruct((batch, MLSTM_HEADS, HEAD_DIM, 2 * HEAD_DIM), f32),
        jax.ShapeDtypeStruct((batch, SUBLANES, LANES), f32),
    )
    out_specs = (
        pl.BlockSpec((1, tt, D_MODEL), lambda b, t: (b, t, 0)),
        pl.BlockSpec((1, HIST_ROWS, POOL_WIDTH), per_b),
        pl.BlockSpec((1, MLSTM_HEADS, HEAD_DIM, 2 * HEAD_DIM), lambda b, t: (b, 0, 0, 0)),
        pl.BlockSpec((1, SUBLANES, LANES), per_b),
    )
    scratch = [
        pltpu.VMEM((tt, D_MODEL), f32),
        pltpu.VMEM((HIST_ROWS + tt, POOL_WIDTH), f32),
        pltpu.VMEM((mp, MLSTM_WIDTH), bf16),
        pltpu.VMEM((MLSTM_WIDTH, mp), f32),
        pltpu.VMEM((mp, 2 * MLSTM_WIDTH), bf16),
        pltpu.VMEM((tt, MLSTM_WIDTH), f32),
        pltpu.VMEM((tt, D_MODEL), bf16),
        pltpu.VMEM((SUBLANES, mp), f32),
        pltpu.VMEM((mp, SUBLANES), f32),
    ]
    return pl.pallas_call(
        kern,
        grid=(batch, n_t),
        in_specs=in_specs,
        out_specs=out_specs,
        out_shape=out_shape,
        scratch_shapes=scratch,
        compiler_params=pltpu.CompilerParams(
            dimension_semantics=("arbitrary", "arbitrary"), vmem_limit_bytes=VMEM_LIMIT_BYTES),
        name="mixer",
    )(x, hist, c0, m0, *wts)


FF_CHUNK = 256


def _ffn_kernel(h_ref, wg_ref, wu_ref, wd_ref, g_ref, b_ref, y_ref, acc_ref, *, alpha):
    h = h_ref[...]
    hb = h.astype(jnp.bfloat16)
    for ci in range(D_FF // FF_CHUNK):
        cs = slice(ci * FF_CHUNK, (ci + 1) * FF_CHUNK)
        gate = _dot(hb, wg_ref[:, cs])
        up = _dot(hb, wu_ref[:, cs])
        part = _dot((jax.nn.silu(gate) * up).astype(jnp.bfloat16), wd_ref[cs, :])
        if ci == 0:
            acc_ref[...] = part
        else:
            acc_ref[...] += part
    y_ref[...] = _layer_norm(alpha * h + acc_ref[...], g_ref[...], b_ref[...])


def _ffn(h, wg, wu, wd, g, b, *, alpha):
    n = h.shape[0]
    tm = min(n, 512)
    assert n % tm == 0
    row_spec = pl.BlockSpec((tm, D_MODEL), lambda i: (i, 0))
    return pl.pallas_call(
        functools.partial(_ffn_kernel, alpha=alpha),
        grid=(n // tm,),
        in_specs=[row_spec] + [_const_spec(w.shape) for w in (wg, wu, wd, g, b)],
        out_specs=row_spec,
        out_shape=jax.ShapeDtypeStruct((n, D_MODEL), jnp.float32),
        scratch_shapes=[pltpu.VMEM((tm, D_MODEL), jnp.float32)],
        compiler_params=pltpu.CompilerParams(
            dimension_semantics=("arbitrary",), vmem_limit_bytes=VMEM_LIMIT_BYTES),
        name="ffn",
    )(h, wg, wu, wd, g, b)


def _mixer_weights(ln_in_g, ln_in_b, w_in, b_in, w_pool, pool_scale, mlstm_norm_g, w_out, ln1_g, ln1_b):
    f32, bf16 = jnp.float32, jnp.bfloat16
    k0, k1, g0 = 2 * 512, 3 * 512, 5 * 512
    w_uqvo = jnp.concatenate([w_in[:, :k0], w_in[:, k1:g0]], axis=1).astype(bf16)
    b_uqvo = jnp.concatenate([b_in[:k0], b_in[k1:g0]]).reshape(1, -1).astype(f32)
    return (
        ln_in_g.reshape(1, -1), ln_in_b.reshape(1, -1), w_uqvo, b_uqvo,
        w_in[:, k0:k1].T.astype(bf16), b_in[k0:k1].reshape(-1, 1).astype(f32),
        w_in[:, g0:].T.astype(bf16), b_in[g0:].reshape(-1, 1).astype(f32),
        w_pool.astype(bf16), pool_scale.reshape(1, -1), mlstm_norm_g.reshape(1, -1), w_out.astype(bf16),
        ln1_g.reshape(1, -1), ln1_b.reshape(1, -1),
    )


def _pack_state(pool, c, n, m):
    batch = pool.shape[0]
    hist = jnp.pad(pool, ((0, 0), (HIST_ROWS - POOL_HIST, 0), (0, 0)))
    n_rep = jnp.broadcast_to(n[..., None], (batch, MLSTM_HEADS, HEAD_DIM, HEAD_DIM))
    c_aug = jnp.concatenate([c, n_rep], axis=-1)
    m_rep = jnp.broadcast_to(m[..., None], (batch, MLSTM_HEADS, LANES))
    m_rep = jnp.pad(m_rep, ((0, 0), (0, SUBLANES - MLSTM_HEADS), (0, 0)))
    return hist, c_aug, m_rep


def _unpack_state(pool_out, c_out, m_out):
    return (pool_out[:, HIST_ROWS - POOL_HIST:], c_out[..., :HEAD_DIM], c_out[..., HEAD_DIM],
            m_out[:, :MLSTM_HEADS, 0])


def kernel(x_prompt, x_sample, state_pool, state_mlstm_C, state_mlstm_n, state_mlstm_m, ln_in_g, ln_in_b, w_in, b_in, w_pool, pool_scale, mlstm_norm_g, w_out, ln1_g, ln1_b, w_gate, w_up, w_down, ln2_g, ln2_b):
    depth = w_in.shape[0]
    past_len = 1024
    alpha = (2.0 * depth) ** 0.25
    f32, bf16 = jnp.float32, jnp.bfloat16
    bp = x_prompt.shape[0]
    zero_state = _pack_state(jnp.zeros((bp, POOL_HIST, POOL_WIDTH), f32),
                             jnp.zeros((bp, MLSTM_HEADS, HEAD_DIM, HEAD_DIM), f32),
                             jnp.zeros((bp, MLSTM_HEADS, HEAD_DIM), f32), jnp.zeros((bp, MLSTM_HEADS), f32))
    hp, hs = x_prompt, x_sample
    acc_p, acc_s = [], []
    for l in range(depth):
        wts = _mixer_weights(ln_in_g, ln_in_b, w_in[l], b_in[l], w_pool[l], pool_scale[l], mlstm_norm_g[l],
                             w_out[l], ln1_g[l], ln1_b[l])
        ffn_w = (w_gate[l].astype(bf16), w_up[l].astype(bf16), w_down[l].astype(bf16),
                 ln2_g[l].reshape(1, -1), ln2_b[l].reshape(1, -1))
        sample_state = _pack_state(state_pool[l], state_mlstm_C[l], state_mlstm_n[l], state_mlstm_m[l])
        outs = []
        for x, state, pos0 in ((hp, zero_state, 0), (hs, sample_state, past_len)):
            h1, pool_o, c_o, m_o = _mixer(x, *state, wts, pos0=pos0, in_ln=(l == 0), alpha=alpha)
            y = _ffn(h1.reshape(-1, D_MODEL), *ffn_w, alpha=alpha).reshape(x.shape)
            outs.append((y, _unpack_state(pool_o, c_o, m_o)))
        (hp, st_p), (hs, st_s) = outs
        acc_p.append(st_p)
        acc_s.append(st_s)
    pool_p, c_p, n_p, m_p = [jnp.stack(a) for a in zip(*acc_p)]
    pool_s, c_s, n_s, m_s = [jnp.stack(a) for a in zip(*acc_s)]
    return (hp, hs, pool_p, c_p, n_p, m_p, pool_s, c_s, n_s, m_s)
```

```python
import functools

import jax
import jax.numpy as jnp
from jax import lax
from jax.experimental import pallas as pl
from jax.experimental.pallas import tpu as pltpu

D_MODEL = 1024
POOL_WIDTH = 512
POOL_WINDOWS = (2, 4, 8, 16)
POOL_GROUP_WIDTH = 128
POOL_HIST = 15
HIST_ROWS = 16
MLSTM_WIDTH = 512
MLSTM_HEADS = 4
HEAD_DIM = 128
D_FF = 2816
FF_CHUNK = 256
LN_EPS = 1e-5
PAST_LEN = 1024

LANES = 128
SUBLANES = 8
VMEM_LIMIT_BYTES = 60 * 1024 * 1024

_NT = (((1,), (1,)), ((), ()))


def _layer_norm(x, g, b):
    mu = jnp.mean(x, axis=-1, keepdims=True)
    xc = x - mu
    var = jnp.mean(xc * xc, axis=-1, keepdims=True)
    return xc * lax.rsqrt(var + LN_EPS) * g + b


def _log_sigmoid(x):
    return -(jnp.maximum(-x, 0.0) + jnp.log1p(jnp.exp(-jnp.abs(x))))


def _dot(a, b):
    return jnp.dot(a, b, preferred_element_type=jnp.float32)


def _ffn_pieces(h, wg_ref, wu_ref, wd_ref, g_ref, b_ref, acc_ref, y_ref, y_idx, alpha):
    hb = h.astype(jnp.bfloat16)
    for ci in range(D_FF // FF_CHUNK):
        cs = slice(ci * FF_CHUNK, (ci + 1) * FF_CHUNK)
        gate = _dot(hb, wg_ref[:, cs])
        up = _dot(hb, wu_ref[:, cs])
        part = _dot((jax.nn.silu(gate) * up).astype(jnp.bfloat16), wd_ref[cs, :])
        if ci == 0:
            acc_ref[...] = part
        else:
            acc_ref[...] += part
        yield
    y_ref[y_idx] = _layer_norm(alpha * h + acc_ref[...], g_ref[...], b_ref[...])
    yield


def _mixer_pieces(res, x, t, hist, c_prev, m_prev, wts, scr, *, tt, mp, pos0, in_ln, alpha):
    (lng_ref, lnb_ref, w_in_ref, b_in_ref, wkT_ref, bkT_ref, wgT_ref, bgT_ref, w_pool_ref, pscale_ref, ng_ref,
     w_out_ref, ln1g_ref, ln1b_ref) = wts
    hp_ref, ext_ref, q_ref, kT_ref, v_ref, o_ref, mix_ref, c_ref, pt_ref = scr
    f32 = jnp.float32
    bf16 = jnp.bfloat16

    hp = _layer_norm(x, lng_ref[...], lnb_ref[...]) if in_ln else x
    hp_ref[...] = hp
    hp_bf = hp.astype(bf16)
    if mp > tt:
        hp_bf = jnp.concatenate([hp_bf, jnp.zeros((mp - tt, D_MODEL), bf16)], axis=0)
    yield "ln"

    gT = lax.dot_general(wgT_ref[...], hp_bf, _NT, preferred_element_type=f32) + bgT_ref[...]

    lane = lax.broadcasted_iota(jnp.int32, (SUBLANES, LANES), 1)
    sub = lax.broadcasted_iota(jnp.int32, (SUBLANES, LANES), 0)
    n_chunks = mp // LANES
    for j in range(n_chunks):
        g8 = gT[:, j * LANES:(j + 1) * LANES]
        b8 = _log_sigmoid(g8)
        k = 1
        while k < LANES:
            b8 = b8 + jnp.where(lane >= k, pltpu.roll(b8, k, 1), 0.0)
            k *= 2
        c8 = g8 - pltpu.roll(b8, MLSTM_HEADS, 0)
        cm8 = c8
        k = 1
        while k < LANES:
            cm8 = jnp.maximum(cm8, jnp.where(lane >= k, pltpu.roll(cm8, k, 1), -jnp.inf))
            k *= 2
        c_ref[:, j * LANES:(j + 1) * LANES] = c8
        pt_ref[j * LANES:(j + 1) * LANES, :] = jnp.where(sub < MLSTM_HEADS, cm8, b8).T
    yield "gates"

    row = lax.broadcasted_iota(jnp.int32, (tt, 1), 0)
    n_avail = pos0 + t * tt + row + 1

    def pool_group(g):
        w = POOL_WINDOWS[g]
        gs = slice(g * POOL_GROUP_WIDTH, (g + 1) * POOL_GROUP_WIDTH)
        tok = ext_ref[HIST_ROWS:HIST_ROWS + tt, gs]
        acc = tok
        for s in range(1, w):
            acc = acc + ext_ref[HIST_ROWS - s:HIST_ROWS - s + tt, gs]
        inv_cnt = 1.0 / jnp.minimum(n_avail, w).astype(f32)
        d = acc * inv_cnt - tok
        p = _dot(d.astype(bf16), w_pool_ref[g]) * pscale_ref[:, gs]
        mix_ref[:, gs] = p.astype(bf16)

    ext_ref[0:HIST_ROWS, :] = hist
    u = _dot(hp_bf, w_in_ref[:, 0:512]) + b_in_ref[:, 0:512]
    ext_ref[HIST_ROWS:HIST_ROWS + tt, :] = u[:tt]
    hist_new = ext_ref[tt:tt + HIST_ROWS, :]
    yield "proj"
    q = _dot(hp_bf, w_in_ref[:, 512:1024]) + b_in_ref[:, 512:1024]
    q_ref[...] = q.astype(bf16)
    pool_group(0)
    pool_group(1)
    yield "proj"
    v = _dot(hp_bf, w_in_ref[:, 1024:1536]) + b_in_ref[:, 1024:1536]
    for h in range(MLSTM_HEADS):
        v_ref[:, h * 256:h * 256 + HEAD_DIM] = v[:, h * HEAD_DIM:(h + 1) * HEAD_DIM].astype(bf16)
    pool_group(2)
    yield "proj"
    o = _dot(hp_bf, w_in_ref[:, 1536:2048]) + b_in_ref[:, 1536:2048]
    o_ref[...] = o[:tt]
    pool_group(3)
    yield "proj"
    kT = lax.dot_general(wkT_ref[...], hp_bf, _NT, preferred_element_type=f32) + bkT_ref[...]
    kT_ref[...] = kT * (HEAD_DIM ** -0.5)
    yield "proj"

    tril = (lax.broadcasted_iota(jnp.int32, (LANES, LANES), 0)
            >= lax.broadcasted_iota(jnp.int32, (LANES, LANES), 1))
    lane_row = lax.broadcasted_iota(jnp.int32, (1, LANES), 1)
    m_state = list(m_prev)
    c_state = list(c_prev)
    for j in range(n_chunks):
        rows = slice(j * LANES, (j + 1) * LANES)
        valid = min(LANES, tt - j * LANES)
        for h in range(MLSTM_HEADS):
            hs = slice(h * HEAD_DIM, (h + 1) * HEAD_DIM)
            qh = q_ref[rows, hs]
            kTh = kT_ref[hs, rows]
            vaug = v_ref[rows, h * 256:(h + 1) * 256]
            c_row = c_ref[h:h + 1, rows]
            cmax = jnp.broadcast_to(pt_ref[rows, h:h + 1], (LANES, LANES))
            bcum = jnp.broadcast_to(pt_ref[rows, MLSTM_HEADS + h:MLSTM_HEADS + h + 1], (LANES, LANES))
            m_row = m_state[h]
            c_aug = c_state[h]

            gmax = jnp.maximum(m_row, cmax)
            dmat = jnp.where(tril, jnp.exp(c_row - gmax), 0.0)
            s = _dot(qh, kTh.astype(bf16)) * dmat
            sv = _dot(s.astype(bf16), vaug)
            qc = _dot(qh, c_aug.astype(bf16))
            dec = jnp.exp(m_row - gmax)
            num = dec * qc[:, :HEAD_DIM] + sv[:, :HEAD_DIM]
            den = dec * qc[:, HEAD_DIM:] + sv[:, HEAD_DIM:]
            hh = num / jnp.maximum(jnp.abs(den), jnp.exp(-(bcum + gmax)))

            mu = jnp.mean(hh, axis=-1, keepdims=True)
            hc = hh - mu
            var = jnp.mean(hc * hc, axis=-1, keepdims=True)
            hn = hc * lax.rsqrt(var + LN_EPS)
            out = hn[:valid] * ng_ref[:, hs] * jax.nn.sigmoid(o_ref[j * LANES:j * LANES + valid, hs])
            mix_ref[j * LANES:j * LANES + valid, POOL_WIDTH + h * HEAD_DIM:POOL_WIDTH + (h + 1) * HEAD_DIM] = (
                out.astype(bf16))

            g_last = gmax[valid - 1:valid, :]
            m_new = bcum[valid - 1:valid, :] + g_last
            w_state = jnp.exp(m_row - g_last)
            w_row = jnp.exp(c_row - g_last)
            if valid < LANES:
                w_row = jnp.where(lane_row < valid, w_row, 0.0)
            kv = _dot((kTh * w_row).astype(bf16), vaug)
            c_state[h] = jnp.concatenate([w_state, w_state], axis=1) * c_aug + kv
            m_state[h] = m_new
        yield "chunk"

    mixo = _dot(mix_ref[...], w_out_ref[...])
    yield "outproj"
    res["h1"] = _layer_norm(alpha * hp_ref[...] + mixo, ln1g_ref[...], ln1b_ref[...])
    res["hist"] = hist_new
    res["c"] = c_state
    res["m"] = m_state
    yield "ln1"


N_MIXER_WEIGHTS = 14
N_FFN_WEIGHTS = 5
N_MIXER_SCRATCH = 9


def _layer_kernel(*refs, tt, mp, n_t, n_tiles, pos0, in_ln, alpha, fuse_ffn):
    x_ref, hist_ref, c0_ref, m0_ref = refs[:4]
    pos = 4
    wts = refs[pos:pos + N_MIXER_WEIGHTS]
    pos += N_MIXER_WEIGHTS
    if fuse_ffn:
        ffn_w = refs[pos:pos + N_FFN_WEIGHTS]
        pos += N_FFN_WEIGHTS
    out_ref, pool_out_ref, c_out_ref, m_out_ref = refs[pos:pos + 4]
    pos += 4
    scr = refs[pos:pos + N_MIXER_SCRATCH]
    pos += N_MIXER_SCRATCH
    hist_st_ref, c_st_ref, m_st_ref = refs[pos:pos + 3]
    pos += 3
    if fuse_ffn:
        h1_ref, acc_ref = refs[pos:pos + 2]

    i = pl.program_id(0)
    live = i < n_tiles
    t = lax.rem(jnp.minimum(i, n_tiles - 1), jnp.int32(n_t))
    first = t == 0
    v_ref = scr[4]

    @pl.when(i == 0)
    def _init():
        for h in range(MLSTM_HEADS):
            v_ref[:, h * 256 + HEAD_DIM:(h + 1) * 256] = jnp.ones((mp, HEAD_DIM), jnp.bfloat16)
        hist_st_ref[...] = jnp.zeros_like(hist_st_ref)
        c_st_ref[...] = jnp.zeros_like(c_st_ref)
        m_st_ref[...] = jnp.zeros_like(m_st_ref)
        if fuse_ffn:
            h1_ref[...] = jnp.zeros_like(h1_ref)

    hist = jnp.where(first, hist_ref[0], hist_st_ref[...])
    c_prev = [jnp.where(first, c0_ref[0, h], c_st_ref[h]) for h in range(MLSTM_HEADS)]
    m_prev = [jnp.where(first, m0_ref[0, h:h + 1, :], m_st_ref[h:h + 1, :]) for h in range(MLSTM_HEADS)]
    res = {}
    mixer = _mixer_pieces(res, x_ref[0], t, hist, c_prev, m_prev, wts, scr, tt=tt, mp=mp, pos0=pos0,
                          in_ln=in_ln, alpha=alpha)
    if fuse_ffn:
        ffn = _ffn_pieces(h1_ref[...], *ffn_w, acc_ref, out_ref, 0, alpha)
        n_ffn = D_FF // FF_CHUNK
        n_chunks = mp // LANES
        done = 0

        def ffn_upto(n):
            nonlocal done
            while done < n:
                next(ffn)
                done += 1

        ffn_upto(1)
        chunk = 0
        for label in mixer:
            if label == "gates":
                ffn_upto(2)
            elif label == "chunk":
                chunk += 1
                ffn_upto(2 + -(-(n_ffn - 2) * chunk // n_chunks))
            elif label == "outproj":
                ffn_upto(n_ffn + 1)
    else:
        for _ in mixer:
            pass
    h1, hist_new, c_new, m_new = res["h1"], res["hist"], res["c"], res["m"]
    if fuse_ffn:
        h1_ref[...] = h1
        hist_new = jnp.where(live, hist_new, hist)
        c_new = [jnp.where(live, cn, cp) for cn, cp in zip(c_new, c_prev)]
        m_new = [jnp.where(live, mn, mp_) for mn, mp_ in zip(m_new, m_prev)]
    else:
        out_ref[0] = h1
    hist_st_ref[...] = hist_new
    pool_out_ref[0] = hist_new
    for h in range(MLSTM_HEADS):
        c_st_ref[h] = c_new[h]
        c_out_ref[0, h] = c_new[h]
        m_st_ref[h:h + 1, :] = m_new[h]
        m_out_ref[0, h:h + 1, :] = m_new[h]
    m_out_ref[0, MLSTM_HEADS:, :] = jnp.zeros((SUBLANES - MLSTM_HEADS, LANES), jnp.float32)


def _const_spec(shape):
    zeros = (0,) * len(shape)
    return pl.BlockSpec(shape, lambda *_: zeros, pipeline_mode=pl.Buffered(1))


def _layer_tiles(seq_len):
    tt = min(seq_len, 512)
    assert seq_len % tt == 0 and tt % HIST_ROWS == 0
    mp = -(-tt // LANES) * LANES
    assert mp == tt or seq_len == tt
    return tt, mp


def _layer(x, hist, c0, m0, wts, ffn_w, *, pos0, in_ln, alpha):
    batch, seq_len, _ = x.shape
    tt, mp = _layer_tiles(seq_len)
    n_t = seq_len // tt
    n_tiles = batch * n_t
    fuse_ffn = ffn_w is not None
    f32, bf16 = jnp.float32, jnp.bfloat16
    kern = functools.partial(_layer_kernel, tt=tt, mp=mp, n_t=n_t, n_tiles=n_tiles, pos0=pos0, in_ln=in_ln,
                             alpha=alpha, fuse_ffn=fuse_ffn)

    def mix_tile(i):
        return jnp.minimum(i, n_tiles - 1)

    def out_tile(i):
        return jnp.maximum(i - 1, 0) if fuse_ffn else i

    row3 = lambda i: (mix_tile(i) // n_t, 0, 0)
    row4 = lambda i: (mix_tile(i) // n_t, 0, 0, 0)
    in_specs = [
        pl.BlockSpec((1, tt, D_MODEL), lambda i: (mix_tile(i) // n_t, mix_tile(i) % n_t, 0)),
        pl.BlockSpec((1, HIST_ROWS, POOL_WIDTH), row3),
        pl.BlockSpec((1, MLSTM_HEADS, HEAD_DIM, 2 * HEAD_DIM), row4),
        pl.BlockSpec((1, SUBLANES, LANES), row3),
    ] + [_const_spec(w.shape) for w in wts + (ffn_w if fuse_ffn else ())]
    out_shape = (
        jax.ShapeDtypeStruct((batch, seq_len, D_MODEL), f32),
        jax.ShapeDtypeStruct((batch, HIST_ROWS, POOL_WIDTH), f32),
        jax.ShapeDtypeStruct((batch, MLSTM_HEADS, HEAD_DIM, 2 * HEAD_DIM), f32),
        jax.ShapeDtypeStruct((batch, SUBLANES, LANES), f32),
    )
    out_specs = (
        pl.BlockSpec((1, tt, D_MODEL), lambda i: (out_tile(i) // n_t, out_tile(i) % n_t, 0)),
        pl.BlockSpec((1, HIST_ROWS, POOL_WIDTH), row3),
        pl.BlockSpec((1, MLSTM_HEADS, HEAD_DIM, 2 * HEAD_DIM), row4),
        pl.BlockSpec((1, SUBLANES, LANES), row3),
    )
    scratch = [
        pltpu.VMEM((tt, D_MODEL), f32),
        pltpu.VMEM((HIST_ROWS + tt, POOL_WIDTH), f32),
        pltpu.VMEM((mp, MLSTM_WIDTH), bf16),
        pltpu.VMEM((MLSTM_WIDTH, mp), f32),
        pltpu.VMEM((mp, 2 * MLSTM_WIDTH), bf16),
        pltpu.VMEM((tt, MLSTM_WIDTH), f32),
        pltpu.VMEM((tt, D_MODEL), bf16),
        pltpu.VMEM((SUBLANES, mp), f32),
        pltpu.VMEM((mp, SUBLANES), f32),
        pltpu.VMEM((HIST_ROWS, POOL_WIDTH), f32),
        pltpu.VMEM((MLSTM_HEADS, HEAD_DIM, 2 * HEAD_DIM), f32),
        pltpu.VMEM((SUBLANES, LANES), f32),
    ]
    if fuse_ffn:
        scratch += [pltpu.VMEM((tt, D_MODEL), f32),
                    pltpu.VMEM((tt, D_MODEL), f32)]
    return pl.pallas_call(
        kern,
        grid=(n_tiles + (1 if fuse_ffn else 0),),
        in_specs=in_specs,
        out_specs=out_specs,
        out_shape=out_shape,
        scratch_shapes=scratch,
        compiler_params=pltpu.CompilerParams(
            dimension_semantics=("arbitrary",), vmem_limit_bytes=VMEM_LIMIT_BYTES),
        name="layer" if fuse_ffn else "mixer",
    )(x, hist, c0, m0, *wts, *(ffn_w if fuse_ffn else ()))


def _ffn_kernel(h_ref, wg_ref, wu_ref, wd_ref, g_ref, b_ref, y_ref, acc_ref, *, alpha):
    for _ in _ffn_pieces(h_ref[...], wg_ref, wu_ref, wd_ref, g_ref, b_ref, acc_ref, y_ref, slice(None), alpha):
        pass


def _ffn(h, wg, wu, wd, g, b, *, alpha):
    n = h.shape[0]
    tm = min(n, 512)
    assert n % tm == 0
    row_spec = pl.BlockSpec((tm, D_MODEL), lambda i: (i, 0))
    return pl.pallas_call(
        functools.partial(_ffn_kernel, alpha=alpha),
        grid=(n // tm,),
        in_specs=[row_spec] + [_const_spec(w.shape) for w in (wg, wu, wd, g, b)],
        out_specs=row_spec,
        out_shape=jax.ShapeDtypeStruct((n, D_MODEL), jnp.float32),
        scratch_shapes=[pltpu.VMEM((tm, D_MODEL), jnp.float32)],
        compiler_params=pltpu.CompilerParams(
            dimension_semantics=("arbitrary",), vmem_limit_bytes=VMEM_LIMIT_BYTES),
        name="ffn",
    )(h, wg, wu, wd, g, b)


def _mixer_weights(ln_in_g, ln_in_b, w_in, b_in, w_pool, pool_scale, mlstm_norm_g, w_out, ln1_g, ln1_b):
    f32, bf16 = jnp.float32, jnp.bfloat16
    k0, k1, g0 = 2 * 512, 3 * 512, 5 * 512
    w_uqvo = jnp.concatenate([w_in[:, :k0], w_in[:, k1:g0]], axis=1).astype(bf16)
    b_uqvo = jnp.concatenate([b_in[:k0], b_in[k1:g0]]).reshape(1, -1).astype(f32)
    return (
        ln_in_g.reshape(1, -1), ln_in_b.reshape(1, -1), w_uqvo, b_uqvo,
        w_in[:, k0:k1].T.astype(bf16), b_in[k0:k1].reshape(-1, 1).astype(f32),
        w_in[:, g0:].T.astype(bf16), b_in[g0:].reshape(-1, 1).astype(f32),
        w_pool.astype(bf16), pool_scale.reshape(1, -1), mlstm_norm_g.reshape(1, -1), w_out.astype(bf16),
        ln1_g.reshape(1, -1), ln1_b.reshape(1, -1),
    )


def _pack_state(pool, c, n, m):
    batch = pool.shape[0]
    hist = jnp.pad(pool, ((0, 0), (HIST_ROWS - POOL_HIST, 0), (0, 0)))
    n_rep = jnp.broadcast_to(n[..., None], (batch, MLSTM_HEADS, HEAD_DIM, HEAD_DIM))
    c_aug = jnp.concatenate([c, n_rep], axis=-1)
    m_rep = jnp.broadcast_to(m[..., None], (batch, MLSTM_HEADS, LANES))
    m_rep = jnp.pad(m_rep, ((0, 0), (0, SUBLANES - MLSTM_HEADS), (0, 0)))
    return hist, c_aug, m_rep


def _unpack_state(pool_out, c_out, m_out):
    return (pool_out[:, HIST_ROWS - POOL_HIST:], c_out[..., :HEAD_DIM], c_out[..., HEAD_DIM],
            m_out[:, :MLSTM_HEADS, 0])


def kernel(x_prompt, x_sample, state_pool, state_mlstm_C, state_mlstm_n, state_mlstm_m, ln_in_g, ln_in_b, w_in, b_in, w_pool, pool_scale, mlstm_norm_g, w_out, ln1_g, ln1_b, w_gate, w_up, w_down, ln2_g, ln2_b):
    depth = w_in.shape[0]
    alpha = (2.0 * depth) ** 0.25
    f32, bf16 = jnp.float32, jnp.bfloat16
    bp = x_prompt.shape[0]
    zero_state = _pack_state(jnp.zeros((bp, POOL_HIST, POOL_WIDTH), f32),
                             jnp.zeros((bp, MLSTM_HEADS, HEAD_DIM, HEAD_DIM), f32),
                             jnp.zeros((bp, MLSTM_HEADS, HEAD_DIM), f32), jnp.zeros((bp, MLSTM_HEADS), f32))
    hp, hs = x_prompt, x_sample
    acc_p, acc_s = [], []
    for l in range(depth):
        wts = _mixer_weights(ln_in_g, ln_in_b, w_in[l], b_in[l], w_pool[l], pool_scale[l], mlstm_norm_g[l],
                             w_out[l], ln1_g[l], ln1_b[l])
        ffn_w = (w_gate[l].astype(bf16), w_up[l].astype(bf16), w_down[l].astype(bf16),
                 ln2_g[l].reshape(1, -1), ln2_b[l].reshape(1, -1))
        sample_state = _pack_state(state_pool[l], state_mlstm_C[l], state_mlstm_n[l], state_mlstm_m[l])
        hp, pool_o, c_o, m_o = _layer(hp, *zero_state, wts, ffn_w, pos0=0, in_ln=(l == 0), alpha=alpha)
        acc_p.append(_unpack_state(pool_o, c_o, m_o))
        h1, pool_o, c_o, m_o = _layer(hs, *sample_state, wts, None, pos0=PAST_LEN, in_ln=(l == 0), alpha=alpha)
        hs = _ffn(h1.reshape(-1, D_MODEL), *ffn_w, alpha=alpha).reshape(hs.shape)
        acc_s.append(_unpack_state(pool_o, c_o, m_o))
    pool_p, c_p, n_p, m_p = [jnp.stack(a) for a in zip(*acc_p)]
    pool_s, c_s, n_s, m_s = [jnp.stack(a) for a in zip(*acc_s)]
    return (hp, hs, pool_p, c_p, n_p, m_p, pool_s, c_s, n_s, m_s)
```

```python
import functools

import jax
import jax.numpy as jnp
from jax import lax
from jax.experimental import pallas as pl
from jax.experimental.pallas import tpu as pltpu

D_MODEL = 1024
POOL_WIDTH = 512
POOL_WINDOWS = (2, 4, 8, 16)
POOL_GROUP_WIDTH = 128
POOL_HIST = 15
HIST_ROWS = 16
MLSTM_WIDTH = 512
MLSTM_HEADS = 4
HEAD_DIM = 128
D_FF = 2816
FF_CHUNK = 256
LN_EPS = 1e-5
PAST_LEN = 1024

LANES = 128
SUBLANES = 8
VMEM_LIMIT_BYTES = 60 * 1024 * 1024

G0 = 5 * 512


def _layer_norm(x, g, b):
    mu = jnp.mean(x, axis=-1, keepdims=True)
    xc = x - mu
    var = jnp.mean(xc * xc, axis=-1, keepdims=True)
    return xc * lax.rsqrt(var + LN_EPS) * g + b


def _log_sigmoid(x):
    return -(jnp.maximum(-x, 0.0) + jnp.log1p(jnp.exp(-jnp.abs(x))))


def _dot(a, b):
    return jnp.dot(a, b, preferred_element_type=jnp.float32)


def _ffn_pieces(h, wg_ref, wu_ref, wd_ref, g_ref, b_ref, acc_ref, y_ref, y_idx, alpha):
    hb = h.astype(jnp.bfloat16)
    for ci in range(D_FF // FF_CHUNK):
        cs = slice(ci * FF_CHUNK, (ci + 1) * FF_CHUNK)
        gate = _dot(hb, wg_ref[:, cs])
        up = _dot(hb, wu_ref[:, cs])
        part = _dot((jax.nn.silu(gate) * up).astype(jnp.bfloat16), wd_ref[cs, :])
        if ci == 0:
            acc_ref[...] = part
        else:
            acc_ref[...] += part
        yield
    y_ref[y_idx] = _layer_norm(alpha * h + acc_ref[...], g_ref[...], b_ref[...])
    yield


def _mixer_pieces(res, x, t, hist, c_prev, m_prev, wts, scr, *, tt, mp, pos0, in_ln, alpha):
    lng_ref, lnb_ref, w_in_ref, b_in_ref, w_pool_ref, pscale_ref, ng_ref, w_out_ref, ln1g_ref, ln1b_ref = wts
    hp_ref, ext_ref, q_ref, k_ref, v_ref, o_ref, mix_ref, c_ref, pt_ref = scr
    f32 = jnp.float32
    bf16 = jnp.bfloat16

    hp = _layer_norm(x, lng_ref[...], lnb_ref[...]) if in_ln else x
    hp_ref[...] = hp
    hp_bf = hp.astype(bf16)
    if mp > tt:
        hp_bf = jnp.concatenate([hp_bf, jnp.zeros((mp - tt, D_MODEL), bf16)], axis=0)
    yield "ln"

    gcol = _dot(hp_bf, w_in_ref[:, G0:G0 + LANES]) + b_in_ref[:, G0:G0 + LANES]

    lane = lax.broadcasted_iota(jnp.int32, (SUBLANES, LANES), 1)
    sub = lax.broadcasted_iota(jnp.int32, (SUBLANES, LANES), 0)
    n_chunks = mp // LANES
    for j in range(n_chunks):
        g8 = gcol[j * LANES:(j + 1) * LANES, :].T[0:SUBLANES, :]
        b8 = _log_sigmoid(g8)
        k = 1
        while k < LANES:
            b8 = b8 + jnp.where(lane >= k, pltpu.roll(b8, k, 1), 0.0)
            k *= 2
        c8 = g8 - pltpu.roll(b8, MLSTM_HEADS, 0)
        cm8 = c8
        k = 1
        while k < LANES:
            cm8 = jnp.maximum(cm8, jnp.where(lane >= k, pltpu.roll(cm8, k, 1), -jnp.inf))
            k *= 2
        c_ref[:, j * LANES:(j + 1) * LANES] = c8
        pt_ref[j * LANES:(j + 1) * LANES, :] = jnp.where(sub < MLSTM_HEADS, cm8, b8).T
    yield "gates"

    row = lax.broadcasted_iota(jnp.int32, (tt, 1), 0)
    n_avail = pos0 + t * tt + row + 1

    def pool_group(g):
        w = POOL_WINDOWS[g]
        gs = slice(g * POOL_GROUP_WIDTH, (g + 1) * POOL_GROUP_WIDTH)
        tok = ext_ref[HIST_ROWS:HIST_ROWS + tt, gs]
        acc = tok
        for s in range(1, w):
            acc = acc + ext_ref[HIST_ROWS - s:HIST_ROWS - s + tt, gs]
        inv_cnt = 1.0 / jnp.minimum(n_avail, w).astype(f32)
        d = acc * inv_cnt - tok
        p = _dot(d.astype(bf16), w_pool_ref[g]) * pscale_ref[:, gs]
        mix_ref[:, gs] = p.astype(bf16)

    ext_ref[0:HIST_ROWS, :] = hist
    u = _dot(hp_bf, w_in_ref[:, 0:512]) + b_in_ref[:, 0:512]
    ext_ref[HIST_ROWS:HIST_ROWS + tt, :] = u[:tt]
    hist_new = ext_ref[tt:tt + HIST_ROWS, :]
    yield "proj"
    q = _dot(hp_bf, w_in_ref[:, 512:1024]) + b_in_ref[:, 512:1024]
    q_ref[...] = q.astype(bf16)
    pool_group(0)
    pool_group(1)
    yield "proj"
    v = _dot(hp_bf, w_in_ref[:, 1536:2048]) + b_in_ref[:, 1536:2048]
    for h in range(MLSTM_HEADS):
        v_ref[:, h * 256:h * 256 + HEAD_DIM] = v[:, h * HEAD_DIM:(h + 1) * HEAD_DIM].astype(bf16)
    pool_group(2)
    yield "proj"
    o = _dot(hp_bf, w_in_ref[:, 2048:2560]) + b_in_ref[:, 2048:2560]
    o_ref[...] = o[:tt]
    pool_group(3)
    yield "proj"
    k = _dot(hp_bf, w_in_ref[:, 1024:1536]) + b_in_ref[:, 1024:1536]
    k_ref[...] = k * (HEAD_DIM ** -0.5)
    yield "proj"

    tril = (lax.broadcasted_iota(jnp.int32, (LANES, LANES), 0)
            >= lax.broadcasted_iota(jnp.int32, (LANES, LANES), 1))
    lane_row = lax.broadcasted_iota(jnp.int32, (1, LANES), 1)
    m_state = list(m_prev)
    c_state = list(c_prev)
    for j in range(n_chunks):
        rows = slice(j * LANES, (j + 1) * LANES)
        valid = min(LANES, tt - j * LANES)
        for h in range(MLSTM_HEADS):
            hs = slice(h * HEAD_DIM, (h + 1) * HEAD_DIM)
            qh = q_ref[rows, hs]
            kTh = k_ref[rows, hs].T
            vaug = v_ref[rows, h * 256:(h + 1) * 256]
            c_row = c_ref[h:h + 1, rows]
            cmax = jnp.broadcast_to(pt_ref[rows, h:h + 1], (LANES, LANES))
            bcum = jnp.broadcast_to(pt_ref[rows, MLSTM_HEADS + h:MLSTM_HEADS + h + 1], (LANES, LANES))
            m_row = m_state[h]
            c_aug = c_state[h]

            gmax = jnp.maximum(m_row, cmax)
            dmat = jnp.where(tril, jnp.exp(c_row - gmax), 0.0)
            s = _dot(qh, kTh.astype(bf16)) * dmat
            sv = _dot(s.astype(bf16), vaug)
            qc = _dot(qh, c_aug.astype(bf16))
            dec = jnp.exp(m_row - gmax)
            num = dec * qc[:, :HEAD_DIM] + sv[:, :HEAD_DIM]
            den = dec * qc[:, HEAD_DIM:] + sv[:, HEAD_DIM:]
            hh = num / jnp.maximum(jnp.abs(den), jnp.exp(-(bcum + gmax)))

            mu = jnp.mean(hh, axis=-1, keepdims=True)
            hc = hh - mu
            var = jnp.mean(hc * hc, axis=-1, keepdims=True)
            hn = hc * lax.rsqrt(var + LN_EPS)
            out = hn[:valid] * ng_ref[:, hs] * jax.nn.sigmoid(o_ref[j * LANES:j * LANES + valid, hs])
            mix_ref[j * LANES:j * LANES + valid, POOL_WIDTH + h * HEAD_DIM:POOL_WIDTH + (h + 1) * HEAD_DIM] = (
                out.astype(bf16))

            g_last = gmax[valid - 1:valid, :]
            m_new = bcum[valid - 1:valid, :] + g_last
            w_state = jnp.exp(m_row - g_last)
            w_row = jnp.exp(c_row - g_last)
            if valid < LANES:
                w_row = jnp.where(lane_row < valid, w_row, 0.0)
            kv = _dot((kTh * w_row).astype(bf16), vaug)
            c_state[h] = jnp.concatenate([w_state, w_state], axis=1) * c_aug + kv
            m_state[h] = m_new
        yield "chunk"

    mixo = _dot(mix_ref[...], w_out_ref[...])
    yield "outproj"
    res["h1"] = _layer_norm(alpha * hp_ref[...] + mixo, ln1g_ref[...], ln1b_ref[...])
    res["hist"] = hist_new
    res["c"] = c_state
    res["m"] = m_state
    yield "ln1"


N_MIXER_WEIGHTS = 10
N_FFN_WEIGHTS = 5
N_MIXER_SCRATCH = 9


def _layer_kernel(*refs, tt, mp, n_t, n_tiles, pos0, in_ln, alpha, fuse_ffn):
    x_ref, hist_ref, c0_ref, m0_ref = refs[:4]
    pos = 4
    wts = refs[pos:pos + N_MIXER_WEIGHTS]
    pos += N_MIXER_WEIGHTS
    if fuse_ffn:
        ffn_w = refs[pos:pos + N_FFN_WEIGHTS]
        pos += N_FFN_WEIGHTS
    out_ref, pool_out_ref, c_out_ref, m_out_ref = refs[pos:pos + 4]
    pos += 4
    scr = refs[pos:pos + N_MIXER_SCRATCH]
    pos += N_MIXER_SCRATCH
    hist_st_ref, c_st_ref, m_st_ref = refs[pos:pos + 3]
    pos += 3
    if fuse_ffn:
        h1_ref, acc_ref = refs[pos:pos + 2]

    i = pl.program_id(0)
    live = i < n_tiles
    t = lax.rem(jnp.minimum(i, n_tiles - 1), jnp.int32(n_t))
    first = t == 0
    v_ref = scr[4]

    @pl.when(i == 0)
    def _init():
        for h in range(MLSTM_HEADS):
            v_ref[:, h * 256 + HEAD_DIM:(h + 1) * 256] = jnp.ones((mp, HEAD_DIM), jnp.bfloat16)
        hist_st_ref[...] = jnp.zeros_like(hist_st_ref)
        c_st_ref[...] = jnp.zeros_like(c_st_ref)
        m_st_ref[...] = jnp.zeros_like(m_st_ref)
        if fuse_ffn:
            h1_ref[...] = jnp.zeros_like(h1_ref)

    hist = jnp.where(first, hist_ref[0], hist_st_ref[...])
    c_prev = [jnp.where(first, c0_ref[0, h], c_st_ref[h]) for h in range(MLSTM_HEADS)]
    m_prev = [jnp.where(first, m0_ref[0, h:h + 1, :], m_st_ref[h:h + 1, :]) for h in range(MLSTM_HEADS)]
    res = {}
    mixer = _mixer_pieces(res, x_ref[0], t, hist, c_prev, m_prev, wts, scr, tt=tt, mp=mp, pos0=pos0,
                          in_ln=in_ln, alpha=alpha)
    if fuse_ffn:
        ffn = _ffn_pieces(h1_ref[...], *ffn_w, acc_ref, out_ref, 0, alpha)
        n_ffn = D_FF // FF_CHUNK
        n_chunks = mp // LANES
        done = 0

        def ffn_upto(n):
            nonlocal done
            while done < n:
                next(ffn)
                done += 1

        ffn_upto(1)
        chunk = 0
        for label in mixer:
            if label == "gates":
                ffn_upto(2)
            elif label == "chunk":
                chunk += 1
                ffn_upto(2 + -(-(n_ffn - 2) * chunk // n_chunks))
            elif label == "outproj":
                ffn_upto(n_ffn + 1)
    else:
        for _ in mixer:
            pass
    h1, hist_new, c_new, m_new = res["h1"], res["hist"], res["c"], res["m"]
    if fuse_ffn:
        h1_ref[...] = h1
        hist_new = jnp.where(live, hist_new, hist)
        c_new = [jnp.where(live, cn, cp) for cn, cp in zip(c_new, c_prev)]
        m_new = [jnp.where(live, mn, mp_) for mn, mp_ in zip(m_new, m_prev)]
    else:
        out_ref[0] = h1
    hist_st_ref[...] = hist_new
    pool_out_ref[0] = hist_new
    for h in range(MLSTM_HEADS):
        c_st_ref[h] = c_new[h]
        c_out_ref[0, h] = c_new[h]
        m_st_ref[h:h + 1, :] = m_new[h]
        m_out_ref[0, h:h + 1, :] = m_new[h]
    m_out_ref[0, MLSTM_HEADS:, :] = jnp.zeros((SUBLANES - MLSTM_HEADS, LANES), jnp.float32)


def _const_spec(shape):
    zeros = (0,) * len(shape)
    return pl.BlockSpec(shape, lambda *_: zeros, pipeline_mode=pl.Buffered(1))


def _layer_tiles(seq_len):
    tt = min(seq_len, 512)
    assert seq_len % tt == 0 and tt % HIST_ROWS == 0
    mp = -(-tt // LANES) * LANES
    assert mp == tt or seq_len == tt
    return tt, mp


def _layer(x, hist, c0, m0, wts, ffn_w, *, pos0, in_ln, alpha):
    batch, seq_len, _ = x.shape
    tt, mp = _layer_tiles(seq_len)
    n_t = seq_len // tt
    n_tiles = batch * n_t
    fuse_ffn = ffn_w is not None
    f32, bf16 = jnp.float32, jnp.bfloat16
    kern = functools.partial(_layer_kernel, tt=tt, mp=mp, n_t=n_t, n_tiles=n_tiles, pos0=pos0, in_ln=in_ln,
                             alpha=alpha, fuse_ffn=fuse_ffn)

    def mix_tile(i):
        return jnp.minimum(i, n_tiles - 1)

    def out_tile(i):
        return jnp.maximum(i - 1, 0) if fuse_ffn else i

    row3 = lambda i: (mix_tile(i) // n_t, 0, 0)
    row4 = lambda i: (mix_tile(i) // n_t, 0, 0, 0)
    in_specs = [
        pl.BlockSpec((1, tt, D_MODEL), lambda i: (mix_tile(i) // n_t, mix_tile(i) % n_t, 0)),
        pl.BlockSpec((1, HIST_ROWS, POOL_WIDTH), row3),
        pl.BlockSpec((1, MLSTM_HEADS, HEAD_DIM, 2 * HEAD_DIM), row4),
        pl.BlockSpec((1, SUBLANES, LANES), row3),
    ] + [_const_spec(w.shape) for w in wts + (ffn_w if fuse_ffn else ())]
    out_shape = (
        jax.ShapeDtypeStruct((batch, seq_len, D_MODEL), f32),
        jax.ShapeDtypeStruct((batch, HIST_ROWS, POOL_WIDTH), f32),
        jax.ShapeDtypeStruct((batch, MLSTM_HEADS, HEAD_DIM, 2 * HEAD_DIM), f32),
        jax.ShapeDtypeStruct((batch, SUBLANES, LANES), f32),
    )
    out_specs = (
        pl.BlockSpec((1, tt, D_MODEL), lambda i: (out_tile(i) // n_t, out_tile(i) % n_t, 0)),
        pl.BlockSpec((1, HIST_ROWS, POOL_WIDTH), row3),
        pl.BlockSpec((1, MLSTM_HEADS, HEAD_DIM, 2 * HEAD_DIM), row4),
        pl.BlockSpec((1, SUBLANES, LANES), row3),
    )
    scratch = [
        pltpu.VMEM((tt, D_MODEL), f32),
        pltpu.VMEM((HIST_ROWS + tt, POOL_WIDTH), f32),
        pltpu.VMEM((mp, MLSTM_WIDTH), bf16),
        pltpu.VMEM((mp, MLSTM_WIDTH), f32),
        pltpu.VMEM((mp, 2 * MLSTM_WIDTH), bf16),
        pltpu.VMEM((tt, MLSTM_WIDTH), f32),
        pltpu.VMEM((tt, D_MODEL), bf16),
        pltpu.VMEM((SUBLANES, mp), f32),
        pltpu.VMEM((mp, SUBLANES), f32),
        pltpu.VMEM((HIST_ROWS, POOL_WIDTH), f32),
        pltpu.VMEM((MLSTM_HEADS, HEAD_DIM, 2 * HEAD_DIM), f32),
        pltpu.VMEM((SUBLANES, LANES), f32),
    ]
    if fuse_ffn:
        scratch += [pltpu.VMEM((tt, D_MODEL), f32),
                    pltpu.VMEM((tt, D_MODEL), f32)]
    return pl.pallas_call(
        kern,
        grid=(n_tiles + (1 if fuse_ffn else 0),),
        in_specs=in_specs,
        out_specs=out_specs,
        out_shape=out_shape,
        scratch_shapes=scratch,
        compiler_params=pltpu.CompilerParams(
            dimension_semantics=("arbitrary",), vmem_limit_bytes=VMEM_LIMIT_BYTES),
        name="layer" if fuse_ffn else "mixer",
    )(x, hist, c0, m0, *wts, *(ffn_w if fuse_ffn else ()))


def _ffn_kernel(h_ref, wg_ref, wu_ref, wd_ref, g_ref, b_ref, y_ref, acc_ref, *, alpha):
    for _ in _ffn_pieces(h_ref[...], wg_ref, wu_ref, wd_ref, g_ref, b_ref, acc_ref, y_ref, slice(None), alpha):
        pass


def _ffn(h, wg, wu, wd, g, b, *, alpha):
    n = h.shape[0]
    tm = min(n, 512)
    assert n % tm == 0
    row_spec = pl.BlockSpec((tm, D_MODEL), lambda i: (i, 0))
    return pl.pallas_call(
        functools.partial(_ffn_kernel, alpha=alpha),
        grid=(n // tm,),
        in_specs=[row_spec] + [_const_spec(w.shape) for w in (wg, wu, wd, g, b)],
        out_specs=row_spec,
        out_shape=jax.ShapeDtypeStruct((n, D_MODEL), jnp.float32),
        scratch_shapes=[pltpu.VMEM((tm, D_MODEL), jnp.float32)],
        compiler_params=pltpu.CompilerParams(
            dimension_semantics=("arbitrary",), vmem_limit_bytes=VMEM_LIMIT_BYTES),
        name="ffn",
    )(h, wg, wu, wd, g, b)


def _mixer_weights(ln_in_g, ln_in_b, w_in, b_in, w_pool, pool_scale, mlstm_norm_g, w_out, ln1_g, ln1_b):
    f32, bf16 = jnp.float32, jnp.bfloat16
    pad = G0 + LANES - w_in.shape[1]
    return (
        ln_in_g.reshape(1, -1), ln_in_b.reshape(1, -1),
        jnp.pad(w_in, ((0, 0), (0, pad))).astype(bf16), jnp.pad(b_in, (0, pad)).reshape(1, -1).astype(f32),
        w_pool.astype(bf16), pool_scale.reshape(1, -1), mlstm_norm_g.reshape(1, -1), w_out.astype(bf16),
        ln1_g.reshape(1, -1), ln1_b.reshape(1, -1),
    )


def _pack_state(pool, c, n, m):
    batch = pool.shape[0]
    hist = jnp.pad(pool, ((0, 0), (HIST_ROWS - POOL_HIST, 0), (0, 0)))
    n_rep = jnp.broadcast_to(n[..., None], (batch, MLSTM_HEADS, HEAD_DIM, HEAD_DIM))
    c_aug = jnp.concatenate([c, n_rep], axis=-1)
    m_rep = jnp.broadcast_to(m[..., None], (batch, MLSTM_HEADS, LANES))
    m_rep = jnp.pad(m_rep, ((0, 0), (0, SUBLANES - MLSTM_HEADS), (0, 0)))
    return hist, c_aug, m_rep


def _unpack_state(pool_out, c_out, m_out):
    return (pool_out[:, HIST_ROWS - POOL_HIST:], c_out[..., :HEAD_DIM], c_out[..., HEAD_DIM],
            m_out[:, :MLSTM_HEADS, 0])


def kernel(x_prompt, x_sample, state_pool, state_mlstm_C, state_mlstm_n, state_mlstm_m, ln_in_g, ln_in_b, w_in, b_in, w_pool, pool_scale, mlstm_norm_g, w_out, ln1_g, ln1_b, w_gate, w_up, w_down, ln2_g, ln2_b):
    depth = w_in.shape[0]
    alpha = (2.0 * depth) ** 0.25
    f32, bf16 = jnp.float32, jnp.bfloat16
    bp = x_prompt.shape[0]
    zero_state = _pack_state(jnp.zeros((bp, POOL_HIST, POOL_WIDTH), f32),
                             jnp.zeros((bp, MLSTM_HEADS, HEAD_DIM, HEAD_DIM), f32),
                             jnp.zeros((bp, MLSTM_HEADS, HEAD_DIM), f32), jnp.zeros((bp, MLSTM_HEADS), f32))
    hp, hs = x_prompt, x_sample
    acc_p, acc_s = [], []
    for l in range(depth):
        wts = _mixer_weights(ln_in_g, ln_in_b, w_in[l], b_in[l], w_pool[l], pool_scale[l], mlstm_norm_g[l],
                             w_out[l], ln1_g[l], ln1_b[l])
        ffn_w = (w_gate[l].astype(bf16), w_up[l].astype(bf16), w_down[l].astype(bf16),
                 ln2_g[l].reshape(1, -1), ln2_b[l].reshape(1, -1))
        sample_state = _pack_state(state_pool[l], state_mlstm_C[l], state_mlstm_n[l], state_mlstm_m[l])
        hp, pool_o, c_o, m_o = _layer(hp, *zero_state, wts, ffn_w, pos0=0, in_ln=(l == 0), alpha=alpha)
        acc_p.append(_unpack_state(pool_o, c_o, m_o))
        h1, pool_o, c_o, m_o = _layer(hs, *sample_state, wts, None, pos0=PAST_LEN, in_ln=(l == 0), alpha=alpha)
        hs = _ffn(h1.reshape(-1, D_MODEL), *ffn_w, alpha=alpha).reshape(hs.shape)
        acc_s.append(_unpack_state(pool_o, c_o, m_o))
    pool_p, c_p, n_p, m_p = [jnp.stack(a) for a in zip(*acc_p)]
    pool_s, c_s, n_s, m_s = [jnp.stack(a) for a in zip(*acc_s)]
    return (hp, hs, pool_p, c_p, n_p, m_p, pool_s, c_s, n_s, m_s)
```

```python
import functools

import jax
import jax.numpy as jnp
from jax import lax
from jax.experimental import pallas as pl
from jax.experimental.pallas import tpu as pltpu

D_MODEL = 1024
POOL_WIDTH = 512
POOL_WINDOWS = (2, 4, 8, 16)
POOL_GROUP_WIDTH = 128
POOL_HIST = 15
HIST_ROWS = 16
MLSTM_WIDTH = 512
MLSTM_HEADS = 4
HEAD_DIM = 128
D_FF = 2816
FF_CHUNK = 256
LN_EPS = 1e-5
PAST_LEN = 1024

LANES = 128
SUBLANES = 8
TILE_ROWS = 512
VMEM_LIMIT_BYTES = 60 * 1024 * 1024

_NT = (((1,), (1,)), ((), ()))


def _layer_norm(x, g, b):
    mu = jnp.mean(x, axis=-1, keepdims=True)
    xc = x - mu
    var = jnp.mean(xc * xc, axis=-1, keepdims=True)
    return xc * lax.rsqrt(var + LN_EPS) * g + b


def _log_sigmoid(x):
    return -(jnp.maximum(-x, 0.0) + jnp.log1p(jnp.exp(-jnp.abs(x))))


def _dot(a, b):
    return jnp.dot(a, b, preferred_element_type=jnp.float32)


def _ffn_pieces(h, wg_ref, wu_ref, wd_ref, g_ref, b_ref, acc_ref, y_ref, y_idx, alpha):
    hb = h.astype(jnp.bfloat16)
    for ci in range(D_FF // FF_CHUNK):
        cs = slice(ci * FF_CHUNK, (ci + 1) * FF_CHUNK)
        gate = _dot(hb, wg_ref[:, cs])
        up = _dot(hb, wu_ref[:, cs])
        yield
        part = _dot((jax.nn.silu(gate) * up).astype(jnp.bfloat16), wd_ref[cs, :])
        if ci == 0:
            acc_ref[...] = part
        else:
            acc_ref[...] += part
        yield
    y_ref[y_idx] = _layer_norm(alpha * h + acc_ref[...], g_ref[...], b_ref[...])
    yield


def _mixer_pieces(res, x, t, hist, c_prev, m_prev, wts, scr, *, nseq, tt, pos0, in_ln, alpha):
    (lng_ref, lnb_ref, w_in_ref, b_in_ref, wkT_ref, bkT_ref, wgT_ref, bgT_ref, w_pool_ref, pscale_ref, ng_ref,
     w_out_ref, ln1g_ref, ln1b_ref) = wts
    hp_ref, ext_ref, q_ref, kT_ref, v_ref, o_ref, mix_ref, c_ref, pt_ref = scr
    f32 = jnp.float32
    bf16 = jnp.bfloat16
    md = nseq * tt
    pp = -(-tt // LANES) * LANES
    mp = nseq * pp
    cps = pp // LANES

    hp = _layer_norm(x, lng_ref[...], lnb_ref[...]) if in_ln else x
    hp_ref[...] = hp
    hp_bf = hp.astype(bf16)
    if pp > tt:
        zpad = jnp.zeros((pp - tt, D_MODEL), bf16)
        hp_bf = jnp.concatenate([a for s in range(nseq) for a in (hp_bf[s * tt:(s + 1) * tt], zpad)], axis=0)
    yield "ln"

    def dense_rows(a):
        if pp == tt:
            return a
        return jnp.concatenate([a[s * pp:s * pp + tt] for s in range(nseq)], axis=0)

    gT = lax.dot_general(wgT_ref[...], hp_bf, _NT, preferred_element_type=f32) + bgT_ref[...]

    lane = lax.broadcasted_iota(jnp.int32, (SUBLANES, LANES), 1)
    sub = lax.broadcasted_iota(jnp.int32, (SUBLANES, LANES), 0)
    n_chunks = mp // LANES
    for j in range(n_chunks):
        g8 = gT[:, j * LANES:(j + 1) * LANES]
        b8 = _log_sigmoid(g8)
        k = 1
        while k < LANES:
            b8 = b8 + jnp.where(lane >= k, pltpu.roll(b8, k, 1), 0.0)
            k *= 2
        c8 = g8 - pltpu.roll(b8, MLSTM_HEADS, 0)
        cm8 = c8
        k = 1
        while k < LANES:
            cm8 = jnp.maximum(cm8, jnp.where(lane >= k, pltpu.roll(cm8, k, 1), -jnp.inf))
            k *= 2
        c_ref[:, j * LANES:(j + 1) * LANES] = c8
        pt_ref[j * LANES:(j + 1) * LANES, :] = jnp.where(sub < MLSTM_HEADS, cm8, b8).T
    yield "gates"

    row = lax.broadcasted_iota(jnp.int32, (tt, 1), 0)
    n_avail = pos0 + t * tt + row + 1

    def pool_group(g):
        w = POOL_WINDOWS[g]
        gs = slice(g * POOL_GROUP_WIDTH, (g + 1) * POOL_GROUP_WIDTH)
        inv_cnt = 1.0 / jnp.minimum(n_avail, w).astype(f32)
        ds = []
        for s in range(nseq):
            tok = ext_ref[s, HIST_ROWS:HIST_ROWS + tt, gs]
            acc = tok
            for sh in range(1, w):
                acc = acc + ext_ref[s, HIST_ROWS - sh:HIST_ROWS - sh + tt, gs]
            ds.append(acc * inv_cnt - tok)
        d = ds[0] if nseq == 1 else jnp.concatenate(ds, axis=0)
        p = _dot(d.astype(bf16), w_pool_ref[g]) * pscale_ref[:, gs]
        mix_ref[:, gs] = p.astype(bf16)

    u = _dot(hp_bf, w_in_ref[:, 0:512]) + b_in_ref[:, 0:512]
    hist_new = []
    for s in range(nseq):
        ext_ref[s, 0:HIST_ROWS, :] = hist[s]
        ext_ref[s, HIST_ROWS:HIST_ROWS + tt, :] = u[s * pp:s * pp + tt]
        hist_new.append(ext_ref[s, tt:tt + HIST_ROWS, :])
    yield "proj"
    q = _dot(hp_bf, w_in_ref[:, 512:1024]) + b_in_ref[:, 512:1024]
    q_ref[...] = q.astype(bf16)
    pool_group(0)
    pool_group(1)
    yield "proj"
    v = _dot(hp_bf, w_in_ref[:, 1024:1536]) + b_in_ref[:, 1024:1536]
    for h in range(MLSTM_HEADS):
        v_ref[:, h * 256:h * 256 + HEAD_DIM] = v[:, h * HEAD_DIM:(h + 1) * HEAD_DIM].astype(bf16)
    pool_group(2)
    yield "proj"
    o = _dot(hp_bf, w_in_ref[:, 1536:2048]) + b_in_ref[:, 1536:2048]
    o_ref[...] = dense_rows(o)
    pool_group(3)
    yield "proj"
    kT = lax.dot_general(wkT_ref[...], hp_bf, _NT, preferred_element_type=f32) + bkT_ref[...]
    kT_ref[...] = kT * (HEAD_DIM ** -0.5)
    yield "proj"

    tril = (lax.broadcasted_iota(jnp.int32, (LANES, LANES), 0)
            >= lax.broadcasted_iota(jnp.int32, (LANES, LANES), 1))
    lane_row = lax.broadcasted_iota(jnp.int32, (1, LANES), 1)
    m_state = [list(m) for m in m_prev]
    c_state = [list(c) for c in c_prev]
    for j in range(n_chunks):
        s, jj = divmod(j, cps)
        rows = slice(j * LANES, (j + 1) * LANES)
        valid = min(LANES, tt - jj * LANES)
        d0 = s * tt + jj * LANES
        for h in range(MLSTM_HEADS):
            hs = slice(h * HEAD_DIM, (h + 1) * HEAD_DIM)
            qh = q_ref[rows, hs]
            kTh = kT_ref[hs, rows]
            vaug = v_ref[rows, h * 256:(h + 1) * 256]
            c_row = c_ref[h:h + 1, rows]
            cmax = jnp.broadcast_to(pt_ref[rows, h:h + 1], (LANES, LANES))
            bcum = jnp.broadcast_to(pt_ref[rows, MLSTM_HEADS + h:MLSTM_HEADS + h + 1], (LANES, LANES))
            m_row = m_state[s][h]
            c_aug = c_state[s][h]

            gmax = jnp.maximum(m_row, cmax)
            dmat = jnp.where(tril, jnp.exp(c_row - gmax), 0.0)
            sc = _dot(qh, kTh.astype(bf16)) * dmat
            sv = _dot(sc.astype(bf16), vaug)
            qc = _dot(qh, c_aug.astype(bf16))
            dec = jnp.exp(m_row - gmax)
            num = dec * qc[:, :HEAD_DIM] + sv[:, :HEAD_DIM]
            den = dec * qc[:, HEAD_DIM:] + sv[:, HEAD_DIM:]
            hh = num / jnp.maximum(jnp.abs(den), jnp.exp(-(bcum + gmax)))

            mu = jnp.mean(hh, axis=-1, keepdims=True)
            hc = hh - mu
            var = jnp.mean(hc * hc, axis=-1, keepdims=True)
            hn = hc * lax.rsqrt(var + LN_EPS)
            out = hn[:valid] * ng_ref[:, hs] * jax.nn.sigmoid(o_ref[d0:d0 + valid, hs])
            mix_ref[d0:d0 + valid, POOL_WIDTH + h * HEAD_DIM:POOL_WIDTH + (h + 1) * HEAD_DIM] = out.astype(bf16)

            g_last = gmax[valid - 1:valid, :]
            m_new = bcum[valid - 1:valid, :] + g_last
            w_state = jnp.exp(m_row - g_last)
            w_row = jnp.exp(c_row - g_last)
            if valid < LANES:
                w_row = jnp.where(lane_row < valid, w_row, 0.0)
            kv = _dot((kTh * w_row).astype(bf16), vaug)
            c_state[s][h] = jnp.concatenate([w_state, w_state], axis=1) * c_aug + kv
            m_state[s][h] = m_new
            yield "head"

    half = md // 2
    if half % SUBLANES or (nseq > 1 and half % tt):
        half = md
    h1_parts = []
    for r0 in range(0, md, half):
        mixo = _dot(mix_ref[r0:r0 + half, :], w_out_ref[...])
        yield "outproj"
        h1_parts.append(_layer_norm(alpha * hp_ref[r0:r0 + half, :] + mixo, ln1g_ref[...], ln1b_ref[...]))
    res["h1"] = h1_parts
    res["hist"] = hist_new
    res["c"] = c_state
    res["m"] = m_state
    yield "ln1"


N_MIXER_WEIGHTS = 14
N_FFN_WEIGHTS = 5
N_MIXER_SCRATCH = 9


def _layer_kernel(*refs, nseq, tt, n_t, n_tiles, pos0, in_ln, alpha, fuse_ffn, zero_init):
    assert nseq == 1 or (n_t == 1 and not fuse_ffn)
    x_ref = refs[0]
    pos = 1
    if not zero_init:
        pool0_ref, c0_ref, n0_ref, m0_ref = refs[pos:pos + 4]
        pos += 4
    wts = refs[pos:pos + N_MIXER_WEIGHTS]
    pos += N_MIXER_WEIGHTS
    if fuse_ffn:
        ffn_w = refs[pos:pos + N_FFN_WEIGHTS]
        pos += N_FFN_WEIGHTS
    out_ref, pool_out_ref, c_out_ref, n_out_ref, m_out_ref = refs[pos:pos + 5]
    pos += 5
    scr = refs[pos:pos + N_MIXER_SCRATCH]
    pos += N_MIXER_SCRATCH
    hist_st_ref, c_st_ref, m_st_ref = refs[pos:pos + 3]
    pos += 3
    if fuse_ffn:
        h1_ref, acc_ref = refs[pos:pos + 2]

    f32 = jnp.float32
    mp = nseq * (-(-tt // LANES) * LANES)
    i = pl.program_id(0)
    live = i < n_tiles
    t = lax.rem(jnp.minimum(i, n_tiles - 1), jnp.int32(n_t))
    first = t == 0
    v_ref = scr[4]

    @pl.when(i == 0)
    def _init():
        for h in range(MLSTM_HEADS):
            v_ref[:, h * 256 + HEAD_DIM:(h + 1) * 256] = jnp.ones((mp, HEAD_DIM), jnp.bfloat16)
        hist_st_ref[...] = jnp.zeros_like(hist_st_ref)
        c_st_ref[...] = jnp.zeros_like(c_st_ref)
        m_st_ref[...] = jnp.zeros_like(m_st_ref)
        if fuse_ffn:
            h1_ref[...] = jnp.zeros_like(h1_ref)

    hist, c_prev, m_prev = [], [], []
    for s in range(nseq):
        if zero_init:
            hist.append(jnp.zeros((HIST_ROWS, POOL_WIDTH), f32))
            c_prev.append([jnp.zeros((HEAD_DIM, 2 * HEAD_DIM), f32)] * MLSTM_HEADS)
            m_prev.append([jnp.zeros((1, LANES), f32)] * MLSTM_HEADS)
        else:
            hist.append(jnp.concatenate(
                [jnp.zeros((HIST_ROWS - POOL_HIST, POOL_WIDTH), f32), pool0_ref[s]], axis=0))
            c_prev.append([jnp.concatenate(
                [c0_ref[s, h], jnp.broadcast_to(n0_ref[s, h], (HEAD_DIM, HEAD_DIM))], axis=1)
                for h in range(MLSTM_HEADS)])
            m_prev.append([jnp.broadcast_to(m0_ref[s, h], (1, LANES)) for h in range(MLSTM_HEADS)])
    if n_t > 1:
        hist = [jnp.where(first, hist[0], hist_st_ref[...])]
        c_prev = [[jnp.where(first, c_prev[0][h], c_st_ref[h]) for h in range(MLSTM_HEADS)]]
        m_prev = [[jnp.where(first, m_prev[0][h], m_st_ref[h:h + 1, :]) for h in range(MLSTM_HEADS)]]

    res = {}
    x = x_ref[...].reshape(nseq * tt, D_MODEL)
    mixer = _mixer_pieces(res, x, t, hist, c_prev, m_prev, wts, scr, nseq=nseq, tt=tt, pos0=pos0,
                          in_ln=in_ln, alpha=alpha)
    if fuse_ffn:
        ffn = _ffn_pieces(h1_ref[...], *ffn_w, acc_ref, out_ref, 0, alpha)
        n_ffn = 2 * (D_FF // FF_CHUNK)
        n_heads = (mp // LANES) * MLSTM_HEADS
        done = 0

        def ffn_upto(n):
            nonlocal done
            while done < n:
                next(ffn)
                done += 1

        ffn_upto(2)
        head = 0
        for label in mixer:
            if label == "gates":
                ffn_upto(4)
            elif label == "head":
                head += 1
                ffn_upto(4 + -(-(n_ffn - 4) * head // n_heads))
            elif label == "outproj":
                ffn_upto(n_ffn + 1)
    else:
        for _ in mixer:
            pass
    h1, hist_new, c_new, m_new = res["h1"], res["hist"], res["c"], res["m"]
    r0 = 0
    for part in h1:
        if fuse_ffn:
            h1_ref[r0:r0 + part.shape[0], :] = part
        elif nseq == 1:
            out_ref[0, r0:r0 + part.shape[0], :] = part
        else:
            out_ref[r0 // tt:(r0 + part.shape[0]) // tt] = part.reshape(part.shape[0] // tt, tt, D_MODEL)
        r0 += part.shape[0]
    if fuse_ffn:
        hist_new = [jnp.where(live, a, b) for a, b in zip(hist_new, hist)]
        c_new = [[jnp.where(live, a, b) for a, b in zip(cn, cp)] for cn, cp in zip(c_new, c_prev)]
        m_new = [[jnp.where(live, a, b) for a, b in zip(mn, mo)] for mn, mo in zip(m_new, m_prev)]
    if n_t > 1:
        hist_st_ref[...] = hist_new[0]
        for h in range(MLSTM_HEADS):
            c_st_ref[h] = c_new[0][h]
            m_st_ref[h:h + 1, :] = m_new[0][h]
    for s in range(nseq):
        pool_out_ref[s] = hist_new[s][HIST_ROWS - POOL_HIST:, :]
        for h in range(MLSTM_HEADS):
            c_out_ref[s, h] = c_new[s][h][:, :HEAD_DIM]
            n_out_ref[s, h] = c_new[s][h][:, HEAD_DIM:HEAD_DIM + 1]
            m_out_ref[s, h] = m_new[s][h][:, 0:1]


def _const_spec(shape):
    zeros = (0,) * len(shape)
    return pl.BlockSpec(shape, lambda *_: zeros, pipeline_mode=pl.Buffered(1))


def _layer_tiles(batch, seq_len):
    if seq_len >= TILE_ROWS:
        assert seq_len % TILE_ROWS == 0
        return 1, TILE_ROWS
    assert seq_len >= HIST_ROWS and seq_len % SUBLANES == 0
    nseq = max(1, TILE_ROWS // (-(-seq_len // LANES) * LANES))
    while batch % nseq:
        nseq -= 1
    return nseq, seq_len


def _layer(x, state, wts, ffn_w, *, pos0, in_ln, alpha):
    batch, seq_len, _ = x.shape
    nseq, tt = _layer_tiles(batch, seq_len)
    n_t = seq_len // tt
    n_tiles = (batch // nseq) * n_t
    md = nseq * tt
    mp = nseq * (-(-tt // LANES) * LANES)
    fuse_ffn = ffn_w is not None
    zero_init = state is None
    f32, bf16 = jnp.float32, jnp.bfloat16
    kern = functools.partial(_layer_kernel, nseq=nseq, tt=tt, n_t=n_t, n_tiles=n_tiles, pos0=pos0, in_ln=in_ln,
                             alpha=alpha, fuse_ffn=fuse_ffn, zero_init=zero_init)

    def mix_tile(i):
        return jnp.minimum(i, n_tiles - 1)

    def out_tile(i):
        return jnp.maximum(i - 1, 0) if fuse_ffn else i

    row3 = lambda i: (mix_tile(i) // n_t, 0, 0)
    row4 = lambda i: (mix_tile(i) // n_t, 0, 0, 0)
    state_specs = [
        pl.BlockSpec((nseq, POOL_HIST, POOL_WIDTH), row3),
        pl.BlockSpec((nseq, MLSTM_HEADS, HEAD_DIM, HEAD_DIM), row4),
        pl.BlockSpec((nseq, MLSTM_HEADS, HEAD_DIM, 1), row4),
        pl.BlockSpec((nseq, MLSTM_HEADS, 1, 1), row4),
    ]
    state_shapes = [
        jax.ShapeDtypeStruct((batch, POOL_HIST, POOL_WIDTH), f32),
        jax.ShapeDtypeStruct((batch, MLSTM_HEADS, HEAD_DIM, HEAD_DIM), f32),
        jax.ShapeDtypeStruct((batch, MLSTM_HEADS, HEAD_DIM, 1), f32),
        jax.ShapeDtypeStruct((batch, MLSTM_HEADS, 1, 1), f32),
    ]
    state_in = ()
    if not zero_init:
        pool, c, n, m = state
        state_in = (pool, c, n.reshape(batch, MLSTM_HEADS, HEAD_DIM, 1), m.reshape(batch, MLSTM_HEADS, 1, 1))
    weights = wts + (ffn_w if fuse_ffn else ())
    in_specs = ([pl.BlockSpec((nseq, tt, D_MODEL), lambda i: (mix_tile(i) // n_t, mix_tile(i) % n_t, 0))]
                + (state_specs if not zero_init else []) + [_const_spec(w.shape) for w in weights])
    out_specs = [pl.BlockSpec((nseq, tt, D_MODEL), lambda i: (out_tile(i) // n_t, out_tile(i) % n_t, 0))
                 ] + state_specs
    out_shape = [jax.ShapeDtypeStruct((batch, seq_len, D_MODEL), f32)] + state_shapes
    scratch = [
        pltpu.VMEM((md, D_MODEL), f32),
        pltpu.VMEM((nseq, HIST_ROWS + tt, POOL_WIDTH), f32),
        pltpu.VMEM((mp, MLSTM_WIDTH), bf16),
        pltpu.VMEM((MLSTM_WIDTH, mp), f32),
        pltpu.VMEM((mp, 2 * MLSTM_WIDTH), bf16),
        pltpu.VMEM((md, MLSTM_WIDTH), f32),
        pltpu.VMEM((md, D_MODEL), bf16),
        pltpu.VMEM((SUBLANES, mp), f32),
        pltpu.VMEM((mp, SUBLANES), f32),
        pltpu.VMEM((HIST_ROWS, POOL_WIDTH), f32),
        pltpu.VMEM((MLSTM_HEADS, HEAD_DIM, 2 * HEAD_DIM), f32),
        pltpu.VMEM((SUBLANES, LANES), f32),
    ]
    if fuse_ffn:
        scratch += [pltpu.VMEM((md, D_MODEL), f32),
                    pltpu.VMEM((md, D_MODEL), f32)]
    out, pool_o, c_o, n_o, m_o = pl.pallas_call(
        kern,
        grid=(n_tiles + (1 if fuse_ffn else 0),),
        in_specs=in_specs,
        out_specs=out_specs,
        out_shape=out_shape,
        scratch_shapes=scratch,
        compiler_params=pltpu.CompilerParams(
            dimension_semantics=("arbitrary",), vmem_limit_bytes=VMEM_LIMIT_BYTES),
        name="layer" if fuse_ffn else "mixer",
    )(x, *state_in, *weights)
    return out, (pool_o, c_o, n_o.reshape(batch, MLSTM_HEADS, HEAD_DIM), m_o.reshape(batch, MLSTM_HEADS))


def _ffn_kernel(h_ref, wg_ref, wu_ref, wd_ref, g_ref, b_ref, y_ref, acc_ref, *, alpha):
    for _ in _ffn_pieces(h_ref[...], wg_ref, wu_ref, wd_ref, g_ref, b_ref, acc_ref, y_ref, slice(None), alpha):
        pass


def _ffn(h, wg, wu, wd, g, b, *, alpha):
    n = h.shape[0]
    tm = min(n, TILE_ROWS)
    assert n % tm == 0
    row_spec = pl.BlockSpec((tm, D_MODEL), lambda i: (i, 0))
    return pl.pallas_call(
        functools.partial(_ffn_kernel, alpha=alpha),
        grid=(n // tm,),
        in_specs=[row_spec] + [_const_spec(w.shape) for w in (wg, wu, wd, g, b)],
        out_specs=row_spec,
        out_shape=jax.ShapeDtypeStruct((n, D_MODEL), jnp.float32),
        scratch_shapes=[pltpu.VMEM((tm, D_MODEL), jnp.float32)],
        compiler_params=pltpu.CompilerParams(
            dimension_semantics=("arbitrary",), vmem_limit_bytes=VMEM_LIMIT_BYTES),
        name="ffn",
    )(h, wg, wu, wd, g, b)


def _mixer_weights(ln_in_g, ln_in_b, w_in, b_in, w_pool, pool_scale, mlstm_norm_g, w_out, ln1_g, ln1_b):
    f32, bf16 = jnp.float32, jnp.bfloat16
    k0, k1, g0 = 2 * 512, 3 * 512, 5 * 512
    w_uqvo = jnp.concatenate([w_in[:, :k0], w_in[:, k1:g0]], axis=1).astype(bf16)
    b_uqvo = jnp.concatenate([b_in[:k0], b_in[k1:g0]]).reshape(1, -1).astype(f32)
    return (
        ln_in_g.reshape(1, -1), ln_in_b.reshape(1, -1), w_uqvo, b_uqvo,
        w_in[:, k0:k1].T.astype(bf16), b_in[k0:k1].reshape(-1, 1).astype(f32),
        w_in[:, g0:].T.astype(bf16), b_in[g0:].reshape(-1, 1).astype(f32),
        w_pool.astype(bf16), pool_scale.reshape(1, -1), mlstm_norm_g.reshape(1, -1), w_out.astype(bf16),
        ln1_g.reshape(1, -1), ln1_b.reshape(1, -1),
    )


def kernel(x_prompt, x_sample, state_pool, state_mlstm_C, state_mlstm_n, state_mlstm_m, ln_in_g, ln_in_b, w_in, b_in, w_pool, pool_scale, mlstm_norm_g, w_out, ln1_g, ln1_b, w_gate, w_up, w_down, ln2_g, ln2_b):
    depth = w_in.shape[0]
    alpha = (2.0 * depth) ** 0.25
    bf16 = jnp.bfloat16
    hp, hs = x_prompt, x_sample
    acc_p, acc_s = [], []
    for l in range(depth):
        wts = _mixer_weights(ln_in_g, ln_in_b, w_in[l], b_in[l], w_pool[l], pool_scale[l], mlstm_norm_g[l],
                             w_out[l], ln1_g[l], ln1_b[l])
        ffn_w = (w_gate[l].astype(bf16), w_up[l].astype(bf16), w_down[l].astype(bf16),
                 ln2_g[l].reshape(1, -1), ln2_b[l].reshape(1, -1))
        hp, st_p = _layer(hp, None, wts, ffn_w, pos0=0, in_ln=(l == 0), alpha=alpha)
        acc_p.append(st_p)
        sample_state = (state_pool[l], state_mlstm_C[l], state_mlstm_n[l], state_mlstm_m[l])
        h1, st_s = _layer(hs, sample_state, wts, None, pos0=PAST_LEN, in_ln=(l == 0), alpha=alpha)
        hs = _ffn(h1.reshape(-1, D_MODEL), *ffn_w, alpha=alpha).reshape(hs.shape)
        acc_s.append(st_s)
    pool_p, c_p, n_p, m_p = [jnp.stack(a) for a in zip(*acc_p)]
    pool_s, c_s, n_s, m_s = [jnp.stack(a) for a in zip(*acc_s)]
    return (hp, hs, pool_p, c_p, n_p, m_p, pool_s, c_s, n_s, m_s)
```

```python
import functools

import jax
import jax.numpy as jnp
from jax import lax
from jax.experimental import pallas as pl
from jax.experimental.pallas import tpu as pltpu

D_MODEL = 1024
POOL_WIDTH = 512
POOL_WINDOWS = (2, 4, 8, 16)
POOL_GROUP_WIDTH = 128
POOL_HIST = 15
HIST_ROWS = 16
MLSTM_WIDTH = 512
MLSTM_HEADS = 4
HEAD_DIM = 128
D_FF = 2816
FF_CHUNK = 256
LN_EPS = 1e-5
PAST_LEN = 1024

LANES = 128
SUBLANES = 8
TILE_ROWS = 512
VMEM_LIMIT_BYTES = 60 * 1024 * 1024

_NT = (((1,), (1,)), ((), ()))


def _layer_norm(x, g, b):
    mu = jnp.mean(x, axis=-1, keepdims=True)
    xc = x - mu
    var = jnp.mean(xc * xc, axis=-1, keepdims=True)
    return xc * lax.rsqrt(var + LN_EPS) * g + b


def _log_sigmoid(x):
    return -(jnp.maximum(-x, 0.0) + jnp.log1p(jnp.exp(-jnp.abs(x))))


def _dot(a, b):
    return jnp.dot(a, b, preferred_element_type=jnp.float32)


N_FF_CHUNKS = D_FF // FF_CHUNK
N_FF_TAIL_CHUNKS = 2


def _ffn_chunks(hb, wg_ref, wu_ref, wd_ref, acc_ref, chunks):
    for ci in chunks:
        cs = slice(ci * FF_CHUNK, (ci + 1) * FF_CHUNK)
        gate = _dot(hb, wg_ref[:, cs])
        up = _dot(hb, wu_ref[:, cs])
        yield
        part = _dot((jax.nn.silu(gate) * up).astype(jnp.bfloat16), wd_ref[cs, :])
        if ci == 0:
            acc_ref[...] = part
        else:
            acc_ref[...] += part
        yield


def _mixer_pieces(res, x, t, hist, c_prev, m_prev, wts, scr, *, nseq, tt, pos0, in_ln, alpha):
    (lng_ref, lnb_ref, w_in_ref, b_in_ref, wkT_ref, bkT_ref, wgT_ref, bgT_ref, w_pool_ref, pscale_ref, ng_ref,
     w_out_ref, ln1g_ref, ln1b_ref) = wts
    hp_ref, ext_ref, q_ref, kT_ref, v_ref, o_ref, mix_ref, c_ref, pt_ref = scr
    f32 = jnp.float32
    bf16 = jnp.bfloat16
    md = nseq * tt
    pp = -(-tt // LANES) * LANES
    mp = nseq * pp
    cps = pp // LANES

    hp = _layer_norm(x, lng_ref[...], lnb_ref[...]) if in_ln else x
    hp_ref[...] = hp
    hp_bf = hp.astype(bf16)
    if pp > tt:
        zpad = jnp.zeros((pp - tt, D_MODEL), bf16)
        hp_bf = jnp.concatenate([a for s in range(nseq) for a in (hp_bf[s * tt:(s + 1) * tt], zpad)], axis=0)
    yield "ln"

    def dense_rows(a):
        if pp == tt:
            return a
        return jnp.concatenate([a[s * pp:s * pp + tt] for s in range(nseq)], axis=0)

    gT = lax.dot_general(wgT_ref[...], hp_bf, _NT, preferred_element_type=f32) + bgT_ref[...]

    lane = lax.broadcasted_iota(jnp.int32, (SUBLANES, LANES), 1)
    sub = lax.broadcasted_iota(jnp.int32, (SUBLANES, LANES), 0)
    n_chunks = mp // LANES
    for j in range(n_chunks):
        g8 = gT[:, j * LANES:(j + 1) * LANES]
        b8 = _log_sigmoid(g8)
        k = 1
        while k < LANES:
            b8 = b8 + jnp.where(lane >= k, pltpu.roll(b8, k, 1), 0.0)
            k *= 2
        c8 = g8 - pltpu.roll(b8, MLSTM_HEADS, 0)
        cm8 = c8
        k = 1
        while k < LANES:
            cm8 = jnp.maximum(cm8, jnp.where(lane >= k, pltpu.roll(cm8, k, 1), -jnp.inf))
            k *= 2
        c_ref[:, j * LANES:(j + 1) * LANES] = c8
        pt_ref[j * LANES:(j + 1) * LANES, :] = jnp.where(sub < MLSTM_HEADS, cm8, b8).T
    yield "gates"

    row = lax.broadcasted_iota(jnp.int32, (tt, 1), 0)
    n_avail = pos0 + t * tt + row + 1

    def pool_group(g):
        w = POOL_WINDOWS[g]
        gs = slice(g * POOL_GROUP_WIDTH, (g + 1) * POOL_GROUP_WIDTH)
        inv_cnt = 1.0 / jnp.minimum(n_avail, w).astype(f32)
        ds = []
        for s in range(nseq):
            tok = ext_ref[s, HIST_ROWS:HIST_ROWS + tt, gs]
            acc = tok
            for sh in range(1, w):
                acc = acc + ext_ref[s, HIST_ROWS - sh:HIST_ROWS - sh + tt, gs]
            ds.append(acc * inv_cnt - tok)
        d = ds[0] if nseq == 1 else jnp.concatenate(ds, axis=0)
        p = _dot(d.astype(bf16), w_pool_ref[g]) * pscale_ref[:, gs]
        mix_ref[:, gs] = p.astype(bf16)

    u = _dot(hp_bf, w_in_ref[:, 0:512]) + b_in_ref[:, 0:512]
    hist_new = []
    for s in range(nseq):
        ext_ref[s, 0:HIST_ROWS, :] = hist[s]
        ext_ref[s, HIST_ROWS:HIST_ROWS + tt, :] = u[s * pp:s * pp + tt]
        hist_new.append(ext_ref[s, tt:tt + HIST_ROWS, :])
    yield "proj"
    q = _dot(hp_bf, w_in_ref[:, 512:1024]) + b_in_ref[:, 512:1024]
    q_ref[...] = q.astype(bf16)
    pool_group(0)
    pool_group(1)
    yield "proj"
    v = _dot(hp_bf, w_in_ref[:, 1024:1536]) + b_in_ref[:, 1024:1536]
    for h in range(MLSTM_HEADS):
        v_ref[:, h * 256:h * 256 + HEAD_DIM] = v[:, h * HEAD_DIM:(h + 1) * HEAD_DIM].astype(bf16)
    pool_group(2)
    yield "proj"
    o = _dot(hp_bf, w_in_ref[:, 1536:2048]) + b_in_ref[:, 1536:2048]
    o_ref[...] = dense_rows(o)
    pool_group(3)
    yield "proj"
    kT = lax.dot_general(wkT_ref[...], hp_bf, _NT, preferred_element_type=f32) + bkT_ref[...]
    kT_ref[...] = kT * (HEAD_DIM ** -0.5)
    yield "proj"

    tril = (lax.broadcasted_iota(jnp.int32, (LANES, LANES), 0)
            >= lax.broadcasted_iota(jnp.int32, (LANES, LANES), 1))
    lane_row = lax.broadcasted_iota(jnp.int32, (1, LANES), 1)
    m_state = [list(m) for m in m_prev]
    c_state = [list(c) for c in c_prev]
    for j in range(n_chunks):
        s, jj = divmod(j, cps)
        rows = slice(j * LANES, (j + 1) * LANES)
        valid = min(LANES, tt - jj * LANES)
        d0 = s * tt + jj * LANES
        for h in range(MLSTM_HEADS):
            hs = slice(h * HEAD_DIM, (h + 1) * HEAD_DIM)
            qh = q_ref[rows, hs]
            kTh = kT_ref[hs, rows]
            vaug = v_ref[rows, h * 256:(h + 1) * 256]
            c_row = c_ref[h:h + 1, rows]
            cmax = jnp.broadcast_to(pt_ref[rows, h:h + 1], (LANES, LANES))
            bcum = jnp.broadcast_to(pt_ref[rows, MLSTM_HEADS + h:MLSTM_HEADS + h + 1], (LANES, LANES))
            m_row = m_state[s][h]
            c_aug = c_state[s][h]

            gmax = jnp.maximum(m_row, cmax)
            dmat = jnp.where(tril, jnp.exp(c_row - gmax), 0.0)
            sc = _dot(qh, kTh.astype(bf16)) * dmat
            sv = _dot(sc.astype(bf16), vaug)
            qc = _dot(qh, c_aug.astype(bf16))
            dec = jnp.exp(m_row - gmax)
            num = dec * qc[:, :HEAD_DIM] + sv[:, :HEAD_DIM]
            den = dec * qc[:, HEAD_DIM:] + sv[:, HEAD_DIM:]
            hh = num / jnp.maximum(jnp.abs(den), jnp.exp(-(bcum + gmax)))

            mu = jnp.mean(hh, axis=-1, keepdims=True)
            hc = hh - mu
            var = jnp.mean(hc * hc, axis=-1, keepdims=True)
            hn = hc * lax.rsqrt(var + LN_EPS)
            out = hn[:valid] * ng_ref[:, hs] * jax.nn.sigmoid(o_ref[d0:d0 + valid, hs])
            mix_ref[d0:d0 + valid, POOL_WIDTH + h * HEAD_DIM:POOL_WIDTH + (h + 1) * HEAD_DIM] = out.astype(bf16)

            g_last = gmax[valid - 1:valid, :]
            m_new = bcum[valid - 1:valid, :] + g_last
            w_state = jnp.exp(m_row - g_last)
            w_row = jnp.exp(c_row - g_last)
            if valid < LANES:
                w_row = jnp.where(lane_row < valid, w_row, 0.0)
            kv = _dot((kTh * w_row).astype(bf16), vaug)
            c_state[s][h] = jnp.concatenate([w_state, w_state], axis=1) * c_aug + kv
            m_state[s][h] = m_new
            yield "head"

    half = md // 2
    if half % SUBLANES or (nseq > 1 and half % tt):
        half = md
    h1_parts = []
    for r0 in range(0, md, half):
        mixo = _dot(mix_ref[r0:r0 + half, :], w_out_ref[...])
        yield "outproj"
        h1_parts.append(_layer_norm(alpha * hp_ref[r0:r0 + half, :] + mixo, ln1g_ref[...], ln1b_ref[...]))
    res["h1"] = h1_parts
    res["hist"] = hist_new
    res["c"] = c_state
    res["m"] = m_state
    yield "ln1"


N_MIXER_WEIGHTS = 14
N_FFN_WEIGHTS = 5
N_MIXER_SCRATCH = 9


def _layer_kernel(*refs, nseq, tt, n_t, n_tiles, pos0, in_ln, alpha, fuse_ffn, zero_init):
    assert nseq == 1 or (n_t == 1 and not fuse_ffn)
    x_ref = refs[0]
    pos = 1
    if not zero_init:
        pool0_ref, c0_ref, n0_ref, m0_ref = refs[pos:pos + 4]
        pos += 4
    wts = refs[pos:pos + N_MIXER_WEIGHTS]
    pos += N_MIXER_WEIGHTS
    if fuse_ffn:
        ffn_w = refs[pos:pos + N_FFN_WEIGHTS]
        pos += N_FFN_WEIGHTS
    out_ref, pool_out_ref, c_out_ref, n_out_ref, m_out_ref = refs[pos:pos + 5]
    pos += 5
    scr = refs[pos:pos + N_MIXER_SCRATCH]
    pos += N_MIXER_SCRATCH
    hist_st_ref, c_st_ref, m_st_ref = refs[pos:pos + 3]
    pos += 3
    if fuse_ffn:
        h1_ref, hb_ref, acc_ref = refs[pos:pos + 3]

    f32 = jnp.float32
    mp = nseq * (-(-tt // LANES) * LANES)
    i = pl.program_id(0)
    live = i < n_tiles
    t = lax.rem(jnp.minimum(i, n_tiles - 1), jnp.int32(n_t))
    first = t == 0
    v_ref = scr[4]

    @pl.when(i == 0)
    def _init():
        for h in range(MLSTM_HEADS):
            v_ref[:, h * 256 + HEAD_DIM:(h + 1) * 256] = jnp.ones((mp, HEAD_DIM), jnp.bfloat16)
        hist_st_ref[...] = jnp.zeros_like(hist_st_ref)
        c_st_ref[...] = jnp.zeros_like(c_st_ref)
        m_st_ref[...] = jnp.zeros_like(m_st_ref)
        if fuse_ffn:
            h1_ref[...] = jnp.zeros_like(h1_ref)
            hb_ref[...] = jnp.zeros_like(hb_ref)
            acc_ref[...] = jnp.zeros_like(acc_ref)

    hist, c_prev, m_prev = [], [], []
    for s in range(nseq):
        if zero_init:
            hist.append(jnp.zeros((HIST_ROWS, POOL_WIDTH), f32))
            c_prev.append([jnp.zeros((HEAD_DIM, 2 * HEAD_DIM), f32)] * MLSTM_HEADS)
            m_prev.append([jnp.zeros((1, LANES), f32)] * MLSTM_HEADS)
        else:
            hist.append(jnp.concatenate(
                [jnp.zeros((HIST_ROWS - POOL_HIST, POOL_WIDTH), f32), pool0_ref[s]], axis=0))
            c_prev.append([jnp.concatenate(
                [c0_ref[s, h], jnp.broadcast_to(n0_ref[s, h], (HEAD_DIM, HEAD_DIM))], axis=1)
                for h in range(MLSTM_HEADS)])
            m_prev.append([jnp.broadcast_to(m0_ref[s, h], (1, LANES)) for h in range(MLSTM_HEADS)])
    if n_t > 1:
        hist = [jnp.where(first, hist[0], hist_st_ref[...])]
        c_prev = [[jnp.where(first, c_prev[0][h], c_st_ref[h]) for h in range(MLSTM_HEADS)]]
        m_prev = [[jnp.where(first, m_prev[0][h], m_st_ref[h:h + 1, :]) for h in range(MLSTM_HEADS)]]

    res = {}
    x = x_ref[...].reshape(nseq * tt, D_MODEL)
    mixer = _mixer_pieces(res, x, t, hist, c_prev, m_prev, wts, scr, nseq=nseq, tt=tt, pos0=pos0,
                          in_ln=in_ln, alpha=alpha)
    if fuse_ffn:
        wg_ref, wu_ref, wd_ref, ln2g_ref, ln2b_ref = ffn_w
        slot = lax.rem(i, jnp.int32(2))
        n_first = N_FF_CHUNKS - N_FF_TAIL_CHUNKS
        tail = _ffn_chunks(hb_ref[...], wg_ref, wu_ref, wd_ref, acc_ref, range(n_first, N_FF_CHUNKS))
        next(tail)
        n_heads = (mp // LANES) * MLSTM_HEADS
        n_body = 2 * n_first
        body = None
        done = 0

        def body_upto(n):
            nonlocal done
            while done < min(n, n_body):
                next(body)
                done += 1

        n_proj = 0
        head = 0
        for label in mixer:
            if label == "ln":
                for _ in range(2 * N_FF_TAIL_CHUNKS - 2):
                    next(tail)
            elif label == "gates":
                next(tail)
            elif label == "proj":
                n_proj += 1
                if n_proj == 1:
                    out_ref[0] = _layer_norm(alpha * h1_ref[slot] + acc_ref[...], ln2g_ref[...], ln2b_ref[...])
                elif n_proj == 5:
                    hb = h1_ref[1 - slot].astype(jnp.bfloat16)
                    hb_ref[...] = hb
                    body = _ffn_chunks(hb, wg_ref, wu_ref, wd_ref, acc_ref, range(n_first))
            elif label == "head":
                head += 1
                body_upto(-(-(n_body - 4) * head // n_heads))
            elif label == "outproj":
                body_upto(done + 2)
        assert n_proj == 5 and done == n_body
    else:
        for _ in mixer:
            pass
    h1, hist_new, c_new, m_new = res["h1"], res["hist"], res["c"], res["m"]
    r0 = 0
    for part in h1:
        if fuse_ffn:
            h1_ref[slot, r0:r0 + part.shape[0], :] = part
        elif nseq == 1:
            out_ref[0, r0:r0 + part.shape[0], :] = part
        else:
            out_ref[r0 // tt:(r0 + part.shape[0]) // tt] = part.reshape(part.shape[0] // tt, tt, D_MODEL)
        r0 += part.shape[0]
    if fuse_ffn:
        hist_new = [jnp.where(live, a, b) for a, b in zip(hist_new, hist)]
        c_new = [[jnp.where(live, a, b) for a, b in zip(cn, cp)] for cn, cp in zip(c_new, c_prev)]
        m_new = [[jnp.where(live, a, b) for a, b in zip(mn, mo)] for mn, mo in zip(m_new, m_prev)]
    if n_t > 1:
        hist_st_ref[...] = hist_new[0]
        for h in range(MLSTM_HEADS):
            c_st_ref[h] = c_new[0][h]
            m_st_ref[h:h + 1, :] = m_new[0][h]
    for s in range(nseq):
        pool_out_ref[s] = hist_new[s][HIST_ROWS - POOL_HIST:, :]
        for h in range(MLSTM_HEADS):
            c_out_ref[s, h] = c_new[s][h][:, :HEAD_DIM]
            n_out_ref[s, h] = c_new[s][h][:, HEAD_DIM:HEAD_DIM + 1]
            m_out_ref[s, h] = m_new[s][h][:, 0:1]


def _const_spec(shape):
    zeros = (0,) * len(shape)
    return pl.BlockSpec(shape, lambda *_: zeros, pipeline_mode=pl.Buffered(1))


def _layer_tiles(batch, seq_len):
    if seq_len >= TILE_ROWS:
        assert seq_len % TILE_ROWS == 0
        return 1, TILE_ROWS
    assert seq_len >= HIST_ROWS and seq_len % SUBLANES == 0
    nseq = max(1, TILE_ROWS // (-(-seq_len // LANES) * LANES))
    while batch % nseq:
        nseq -= 1
    return nseq, seq_len


def _layer(x, state, wts, ffn_w, *, pos0, in_ln, alpha):
    batch, seq_len, _ = x.shape
    nseq, tt = _layer_tiles(batch, seq_len)
    n_t = seq_len // tt
    n_tiles = (batch // nseq) * n_t
    md = nseq * tt
    mp = nseq * (-(-tt // LANES) * LANES)
    fuse_ffn = ffn_w is not None
    zero_init = state is None
    f32, bf16 = jnp.float32, jnp.bfloat16
    kern = functools.partial(_layer_kernel, nseq=nseq, tt=tt, n_t=n_t, n_tiles=n_tiles, pos0=pos0, in_ln=in_ln,
                             alpha=alpha, fuse_ffn=fuse_ffn, zero_init=zero_init)

    def mix_tile(i):
        return jnp.minimum(i, n_tiles - 1)

    def out_tile(i):
        return jnp.maximum(i - 2, 0) if fuse_ffn else i

    row3 = lambda i: (mix_tile(i) // n_t, 0, 0)
    row4 = lambda i: (mix_tile(i) // n_t, 0, 0, 0)
    state_specs = [
        pl.BlockSpec((nseq, POOL_HIST, POOL_WIDTH), row3),
        pl.BlockSpec((nseq, MLSTM_HEADS, HEAD_DIM, HEAD_DIM), row4),
        pl.BlockSpec((nseq, MLSTM_HEADS, HEAD_DIM, 1), row4),
        pl.BlockSpec((nseq, MLSTM_HEADS, 1, 1), row4),
    ]
    state_shapes = [
        jax.ShapeDtypeStruct((batch, POOL_HIST, POOL_WIDTH), f32),
        jax.ShapeDtypeStruct((batch, MLSTM_HEADS, HEAD_DIM, HEAD_DIM), f32),
        jax.ShapeDtypeStruct((batch, MLSTM_HEADS, HEAD_DIM, 1), f32),
        jax.ShapeDtypeStruct((batch, MLSTM_HEADS, 1, 1), f32),
    ]
    state_in = ()
    if not zero_init:
        pool, c, n, m = state
        state_in = (pool, c, n.reshape(batch, MLSTM_HEADS, HEAD_DIM, 1), m.reshape(batch, MLSTM_HEADS, 1, 1))
    weights = wts + (ffn_w if fuse_ffn else ())
    in_specs = ([pl.BlockSpec((nseq, tt, D_MODEL), lambda i: (mix_tile(i) // n_t, mix_tile(i) % n_t, 0))]
                + (state_specs if not zero_init else []) + [_const_spec(w.shape) for w in weights])
    out_specs = [pl.BlockSpec((nseq, tt, D_MODEL), lambda i: (out_tile(i) // n_t, out_tile(i) % n_t, 0))
                 ] + state_specs
    out_shape = [jax.ShapeDtypeStruct((batch, seq_len, D_MODEL), f32)] + state_shapes
    scratch = [
        pltpu.VMEM((md, D_MODEL), f32),
        pltpu.VMEM((nseq, HIST_ROWS + tt, POOL_WIDTH), f32),
        pltpu.VMEM((mp, MLSTM_WIDTH), bf16),
        pltpu.VMEM((MLSTM_WIDTH, mp), f32),
        pltpu.VMEM((mp, 2 * MLSTM_WIDTH), bf16),
        pltpu.VMEM((md, MLSTM_WIDTH), f32),
        pltpu.VMEM((md, D_MODEL), bf16),
        pltpu.VMEM((SUBLANES, mp), f32),
        pltpu.VMEM((mp, SUBLANES), f32),
        pltpu.VMEM((HIST_ROWS, POOL_WIDTH), f32),
        pltpu.VMEM((MLSTM_HEADS, HEAD_DIM, 2 * HEAD_DIM), f32),
        pltpu.VMEM((SUBLANES, LANES), f32),
    ]
    if fuse_ffn:
        scratch += [pltpu.VMEM((2, md, D_MODEL), f32),
                    pltpu.VMEM((md, D_MODEL), bf16),
                    pltpu.VMEM((md, D_MODEL), f32)]
    out, pool_o, c_o, n_o, m_o = pl.pallas_call(
        kern,
        grid=(n_tiles + (2 if fuse_ffn else 0),),
        in_specs=in_specs,
        out_specs=out_specs,
        out_shape=out_shape,
        scratch_shapes=scratch,
        compiler_params=pltpu.CompilerParams(
            dimension_semantics=("arbitrary",), vmem_limit_bytes=VMEM_LIMIT_BYTES),
        name="layer" if fuse_ffn else "mixer",
    )(x, *state_in, *weights)
    return out, (pool_o, c_o, n_o.reshape(batch, MLSTM_HEADS, HEAD_DIM), m_o.reshape(batch, MLSTM_HEADS))


def _ffn_kernel(h_ref, wg_ref, wu_ref, wd_ref, g_ref, b_ref, y_ref, acc_ref, *, alpha):
    h = h_ref[...]
    for _ in _ffn_chunks(h.astype(jnp.bfloat16), wg_ref, wu_ref, wd_ref, acc_ref, range(N_FF_CHUNKS)):
        pass
    y_ref[...] = _layer_norm(alpha * h + acc_ref[...], g_ref[...], b_ref[...])


def _ffn(h, wg, wu, wd, g, b, *, alpha):
    n = h.shape[0]
    tm = min(n, TILE_ROWS)
    assert n % tm == 0
    row_spec = pl.BlockSpec((tm, D_MODEL), lambda i: (i, 0))
    return pl.pallas_call(
        functools.partial(_ffn_kernel, alpha=alpha),
        grid=(n // tm,),
        in_specs=[row_spec] + [_const_spec(w.shape) for w in (wg, wu, wd, g, b)],
        out_specs=row_spec,
        out_shape=jax.ShapeDtypeStruct((n, D_MODEL), jnp.float32),
        scratch_shapes=[pltpu.VMEM((tm, D_MODEL), jnp.float32)],
        compiler_params=pltpu.CompilerParams(
            dimension_semantics=("arbitrary",), vmem_limit_bytes=VMEM_LIMIT_BYTES),
        name="ffn",
    )(h, wg, wu, wd, g, b)


def _mixer_weights(ln_in_g, ln_in_b, w_in, b_in, w_pool, pool_scale, mlstm_norm_g, w_out, ln1_g, ln1_b):
    f32, bf16 = jnp.float32, jnp.bfloat16
    k0, k1, g0 = 2 * 512, 3 * 512, 5 * 512
    w_uqvo = jnp.concatenate([w_in[:, :k0], w_in[:, k1:g0]], axis=1).astype(bf16)
    b_uqvo = jnp.concatenate([b_in[:k0], b_in[k1:g0]]).reshape(1, -1).astype(f32)
    return (
        ln_in_g.reshape(1, -1), ln_in_b.reshape(1, -1), w_uqvo, b_uqvo,
        w_in[:, k0:k1].T.astype(bf16), b_in[k0:k1].reshape(-1, 1).astype(f32),
        w_in[:, g0:].T.astype(bf16), b_in[g0:].reshape(-1, 1).astype(f32),
        w_pool.astype(bf16), pool_scale.reshape(1, -1), mlstm_norm_g.reshape(1, -1), w_out.astype(bf16),
        ln1_g.reshape(1, -1), ln1_b.reshape(1, -1),
    )


def kernel(x_prompt, x_sample, state_pool, state_mlstm_C, state_mlstm_n, state_mlstm_m, ln_in_g, ln_in_b, w_in, b_in, w_pool, pool_scale, mlstm_norm_g, w_out, ln1_g, ln1_b, w_gate, w_up, w_down, ln2_g, ln2_b):
    depth = w_in.shape[0]
    alpha = (2.0 * depth) ** 0.25
    bf16 = jnp.bfloat16
    hp, hs = x_prompt, x_sample
    acc_p, acc_s = [], []
    for l in range(depth):
        wts = _mixer_weights(ln_in_g, ln_in_b, w_in[l], b_in[l], w_pool[l], pool_scale[l], mlstm_norm_g[l],
                             w_out[l], ln1_g[l], ln1_b[l])
        ffn_w = (w_gate[l].astype(bf16), w_up[l].astype(bf16), w_down[l].astype(bf16),
                 ln2_g[l].reshape(1, -1), ln2_b[l].reshape(1, -1))
        hp, st_p = _layer(hp, None, wts, ffn_w, pos0=0, in_ln=(l == 0), alpha=alpha)
        acc_p.append(st_p)
        sample_state = (state_pool[l], state_mlstm_C[l], state_mlstm_n[l], state_mlstm_m[l])
        h1, st_s = _layer(hs, sample_state, wts, None, pos0=PAST_LEN, in_ln=(l == 0), alpha=alpha)
        hs = _ffn(h1.reshape(-1, D_MODEL), *ffn_w, alpha=alpha).reshape(hs.shape)
        acc_s.append(st_s)
    pool_p, c_p, n_p, m_p = [jnp.stack(a) for a in zip(*acc_p)]
    pool_s, c_s, n_s, m_s = [jnp.stack(a) for a in zip(*acc_s)]
    return (hp, hs, pool_p, c_p, n_p, m_p, pool_s, c_s, n_s, m_s)
```

```python
import functools

import jax
import jax.numpy as jnp
from jax import lax
from jax.experimental import pallas as pl
from jax.experimental.pallas import tpu as pltpu

D_MODEL = 1024
POOL_WIDTH = 512
POOL_WINDOWS = (2, 4, 8, 16)
POOL_GROUP_WIDTH = 128
POOL_HIST = 15
HIST_ROWS = 16
MLSTM_WIDTH = 512
MLSTM_HEADS = 4
HEAD_DIM = 128
D_FF = 2816
FF_CHUNK = 256
LN_EPS = 1e-5
PAST_LEN = 1024

LANES = 128
SUBLANES = 8
TILE_ROWS = 512
VMEM_LIMIT_BYTES = 60 * 1024 * 1024

_NT = (((1,), (1,)), ((), ()))


def _layer_norm(x, g, b):
    mu = jnp.mean(x, axis=-1, keepdims=True)
    xc = x - mu
    var = jnp.mean(xc * xc, axis=-1, keepdims=True)
    return xc * lax.rsqrt(var + LN_EPS) * g + b


def _log_sigmoid(x):
    return -(jnp.maximum(-x, 0.0) + jnp.log1p(jnp.exp(-jnp.abs(x))))


def _dot(a, b):
    return jnp.dot(a, b, preferred_element_type=jnp.float32)


N_FF_CHUNKS = D_FF // FF_CHUNK
N_FF_TAIL_CHUNKS = 2


def _ffn_chunks(hb, wg_ref, wu_ref, wd_ref, acc_ref, chunks):
    for ci in chunks:
        cs = slice(ci * FF_CHUNK, (ci + 1) * FF_CHUNK)
        gate = _dot(hb, wg_ref[:, cs])
        up = _dot(hb, wu_ref[:, cs])
        yield
        part = _dot((jax.nn.silu(gate) * up).astype(jnp.bfloat16), wd_ref[cs, :])
        if ci == 0:
            acc_ref[...] = part
        else:
            acc_ref[...] += part
        yield


N_ROWS = 16


def _mixer_pieces(res, x, t, hist, ct_prev, n_prev, m_prev, wts, scr, *, nseq, tt, pos0, in_ln, alpha):
    (lng_ref, lnb_ref, w_uk_ref, b_uk_ref, wT_ref, bT_ref, w_pool_ref, pscale_ref, ngb_ref, w_out_ref,
     ln1g_ref, ln1b_ref) = wts
    hp_ref, ext_ref, qT_ref, k_ref, kb_ref, vT_ref, oT_ref, mix_ref, c_ref, r_ref, pt_ref = scr
    f32 = jnp.float32
    bf16 = jnp.bfloat16
    md = nseq * tt
    pp = -(-tt // LANES) * LANES
    mp = nseq * pp
    cps = pp // LANES
    n_chunks = mp // LANES

    def valid_rows(j):
        return min(LANES, tt - (j % cps) * LANES)

    def proj_t(r0, r1):
        return (lax.dot_general(wT_ref[r0:r1, :], hp_bf, _NT, preferred_element_type=f32) + bT_ref[r0:r1, :])

    hp = _layer_norm(x, lng_ref[...], lnb_ref[...]) if in_ln else x
    hp_ref[...] = hp
    hp_bf = hp.astype(bf16)
    if pp > tt:
        zpad = jnp.zeros((pp - tt, D_MODEL), bf16)
        hp_bf = jnp.concatenate([a for s in range(nseq) for a in (hp_bf[s * tt:(s + 1) * tt], zpad)], axis=0)
    yield "ln"

    gT = proj_t(3 * MLSTM_WIDTH, 3 * MLSTM_WIDTH + 2 * SUBLANES)[0:SUBLANES]

    lane = lax.broadcasted_iota(jnp.int32, (SUBLANES, LANES), 1)
    sub = lax.broadcasted_iota(jnp.int32, (SUBLANES, LANES), 0)
    head_rows = sub < MLSTM_HEADS
    for j in range(n_chunks):
        cols = slice(j * LANES, (j + 1) * LANES)
        last = valid_rows(j) - 1
        g8 = gT[:, cols]
        b8 = _log_sigmoid(g8)
        k = 1
        while k < LANES:
            b8 = b8 + jnp.where(lane >= k, pltpu.roll(b8, k, 1), 0.0)
            k *= 2
        b8 = pltpu.roll(b8, MLSTM_HEADS, 0)
        c8 = jnp.where(head_rows, g8 - b8, 0.0)
        b8 = jnp.where(head_rows, b8, 0.0)
        cm8 = c8
        k = 1
        while k < LANES:
            cm8 = jnp.maximum(cm8, jnp.where(lane >= k, pltpu.roll(cm8, k, 1), -jnp.inf))
            k *= 2
        c_ref[:, cols] = c8
        r_ref[0, :, cols] = cm8
        r_ref[1, :, cols] = b8
        r_ref[2, :, cols] = jnp.broadcast_to(cm8[:, last:last + 1], (SUBLANES, LANES))
        r_ref[3, :, cols] = jnp.broadcast_to(b8[:, last:last + 1], (SUBLANES, LANES))
        pt_ref[cols, :] = c8.T
    yield "gates"

    row = lax.broadcasted_iota(jnp.int32, (tt, 1), 0)
    n_avail = pos0 + t * tt + row + 1

    def pool_group(g):
        w = POOL_WINDOWS[g]
        gs = slice(g * POOL_GROUP_WIDTH, (g + 1) * POOL_GROUP_WIDTH)
        inv_cnt = 1.0 / jnp.minimum(n_avail, w).astype(f32)
        ds = []
        for s in range(nseq):
            tok = ext_ref[s, HIST_ROWS:HIST_ROWS + tt, gs]
            acc = tok
            for sh in range(1, w):
                acc = acc + ext_ref[s, HIST_ROWS - sh:HIST_ROWS - sh + tt, gs]
            ds.append(acc * inv_cnt - tok)
        d = ds[0] if nseq == 1 else jnp.concatenate(ds, axis=0)
        p = _dot(d.astype(bf16), w_pool_ref[g]) * pscale_ref[:, gs]
        mix_ref[:, gs] = p.astype(bf16)

    u = _dot(hp_bf, w_uk_ref[:, 0:512]) + b_uk_ref[:, 0:512]
    hist_new = []
    for s in range(nseq):
        ext_ref[s, 0:HIST_ROWS, :] = hist[s]
        ext_ref[s, HIST_ROWS:HIST_ROWS + tt, :] = u[s * pp:s * pp + tt]
        hist_new.append(ext_ref[s, tt:tt + HIST_ROWS, :])
    yield "proj"
    qT_ref[...] = proj_t(0, MLSTM_WIDTH).astype(bf16)
    pool_group(0)
    pool_group(1)
    yield "proj"
    kk = (_dot(hp_bf, w_uk_ref[:, 512:1024]) + b_uk_ref[:, 512:1024]) * (HEAD_DIM ** -0.5)
    k_ref[...] = kk
    kb_ref[...] = kk.astype(bf16)
    pool_group(2)
    yield "proj"
    vT_ref[...] = proj_t(MLSTM_WIDTH, 2 * MLSTM_WIDTH).astype(bf16)
    pool_group(3)
    yield "proj"
    oT_ref[...] = proj_t(2 * MLSTM_WIDTH, 3 * MLSTM_WIDTH)
    yield "proj"

    causal = (lax.broadcasted_iota(jnp.int32, (LANES, LANES), 0)
              <= lax.broadcasted_iota(jnp.int32, (LANES, LANES), 1))
    s_idx = lax.broadcasted_iota(jnp.int32, (LANES, LANES), 0)
    m_state = list(m_prev)
    ct_state = [list(c) for c in ct_prev]
    n_state = [list(n) for n in n_prev]
    for j in range(n_chunks):
        sq = j // cps
        cols = slice(j * LANES, (j + 1) * LANES)
        valid = valid_rows(j)
        d0 = sq * tt + (j % cps) * LANES
        m8 = m_state[sq]
        g8 = jnp.maximum(m8, r_ref[0, :, cols])
        dec8 = jnp.exp(m8 - g8)
        e8 = jnp.exp(-(r_ref[1, :, cols] + g8))
        gl8 = jnp.maximum(m8, r_ref[2, :, cols])
        m_state[sq] = r_ref[3, :, cols] + gl8
        ws8 = jnp.exp(m8 - gl8)
        for h in range(MLSTM_HEADS):
            hs = slice(h * HEAD_DIM, (h + 1) * HEAD_DIM)
            ct = ct_state[sq][h]
            n_row = n_state[sq][h]
            q_t = qT_ref[hs, cols]
            v_t = vT_ref[hs, cols]
            c_col = jnp.broadcast_to(pt_ref[cols, h:h + 1], (LANES, LANES))
            dmat = jnp.where(causal, jnp.exp(c_col - g8[h:h + 1]), 0.0)
            st = _dot(kb_ref[cols, hs], q_t) * dmat
            w_col = jnp.exp(c_col - gl8[h:h + 1])
            if valid < LANES:
                w_col = jnp.where(s_idx < valid, w_col, 0.0)
            kw = k_ref[cols, hs] * w_col
            n16 = jnp.concatenate([n_row, jnp.zeros((N_ROWS - 1, HEAD_DIM), f32)], axis=0).astype(bf16)
            dec = dec8[h:h + 1]
            num = dec * _dot(ct.astype(bf16), q_t) + _dot(v_t, st.astype(bf16))
            den = dec * _dot(n16, q_t)[0:1] + jnp.sum(st, axis=0, keepdims=True)
            hh = num * (1.0 / jnp.maximum(jnp.abs(den), e8[h:h + 1]))

            mu = jnp.mean(hh, axis=0, keepdims=True)
            hc = hh - mu
            var = jnp.mean(hc * hc, axis=0, keepdims=True)
            hn = hc * lax.rsqrt(var + LN_EPS)
            out_t = hn * ngb_ref[hs, :] * jax.nn.sigmoid(oT_ref[hs, cols])
            mix_ref[d0:d0 + valid, POOL_WIDTH + h * HEAD_DIM:POOL_WIDTH + (h + 1) * HEAD_DIM] = (
                out_t.T[:valid].astype(bf16))

            ct_state[sq][h] = ws8[h:h + 1] * ct + _dot(v_t, kw.astype(bf16))
            n_state[sq][h] = ws8[h:h + 1] * n_row + jnp.sum(kw, axis=0, keepdims=True)
            yield "head"

    half = md // 2
    if half % SUBLANES or (nseq > 1 and half % tt):
        half = md
    h1_parts = []
    for r0 in range(0, md, half):
        mixo = _dot(mix_ref[r0:r0 + half, :], w_out_ref[...])
        yield "outproj"
        h1_parts.append(_layer_norm(alpha * hp_ref[r0:r0 + half, :] + mixo, ln1g_ref[...], ln1b_ref[...]))
    res["h1"] = h1_parts
    res["hist"] = hist_new
    res["ct"] = ct_state
    res["n"] = n_state
    res["m"] = m_state
    yield "ln1"


N_MIXER_WEIGHTS = 12
N_FFN_WEIGHTS = 5
N_MIXER_SCRATCH = 11


def _layer_kernel(*refs, nseq, tt, n_t, n_tiles, pos0, in_ln, alpha, fuse_ffn, zero_init):
    assert nseq == 1 or (n_t == 1 and not fuse_ffn)
    x_ref = refs[0]
    pos = 1
    if not zero_init:
        pool0_ref, c0_ref, n0_ref, m0_ref = refs[pos:pos + 4]
        pos += 4
    wts = refs[pos:pos + N_MIXER_WEIGHTS]
    pos += N_MIXER_WEIGHTS
    if fuse_ffn:
        ffn_w = refs[pos:pos + N_FFN_WEIGHTS]
        pos += N_FFN_WEIGHTS
    out_ref, pool_out_ref, c_out_ref, n_out_ref, m_out_ref = refs[pos:pos + 5]
    pos += 5
    scr = refs[pos:pos + N_MIXER_SCRATCH]
    pos += N_MIXER_SCRATCH
    hist_st_ref, ct_st_ref, n_st_ref, m_st_ref = refs[pos:pos + 4]
    pos += 4
    if fuse_ffn:
        h1_ref, hb_ref, acc_ref = refs[pos:pos + 3]

    f32 = jnp.float32
    mp = nseq * (-(-tt // LANES) * LANES)
    i = pl.program_id(0)
    live = i < n_tiles
    t = lax.rem(jnp.minimum(i, n_tiles - 1), jnp.int32(n_t))
    first = t == 0

    @pl.when(i == 0)
    def _init():
        hist_st_ref[...] = jnp.zeros_like(hist_st_ref)
        ct_st_ref[...] = jnp.zeros_like(ct_st_ref)
        n_st_ref[...] = jnp.zeros_like(n_st_ref)
        m_st_ref[...] = jnp.zeros_like(m_st_ref)
        if fuse_ffn:
            h1_ref[...] = jnp.zeros_like(h1_ref)
            hb_ref[...] = jnp.zeros_like(hb_ref)
            acc_ref[...] = jnp.zeros_like(acc_ref)

    sub = lax.broadcasted_iota(jnp.int32, (SUBLANES, LANES), 0)
    hist, ct_prev, n_prev, m_prev = [], [], [], []
    for s in range(nseq):
        if zero_init:
            hist.append(jnp.zeros((HIST_ROWS, POOL_WIDTH), f32))
            ct_prev.append([jnp.zeros((HEAD_DIM, HEAD_DIM), f32)] * MLSTM_HEADS)
            n_prev.append([jnp.zeros((1, HEAD_DIM), f32)] * MLSTM_HEADS)
            m_prev.append(jnp.zeros((SUBLANES, LANES), f32))
        else:
            hist.append(jnp.concatenate(
                [jnp.zeros((HIST_ROWS - POOL_HIST, POOL_WIDTH), f32), pool0_ref[s]], axis=0))
            ct_prev.append([c0_ref[s, h].T for h in range(MLSTM_HEADS)])
            n_prev.append([n0_ref[s, h] for h in range(MLSTM_HEADS)])
            m8 = jnp.zeros((SUBLANES, LANES), f32)
            for h in range(MLSTM_HEADS):
                m8 = jnp.where(sub == h, jnp.broadcast_to(m0_ref[s, h], (SUBLANES, LANES)), m8)
            m_prev.append(m8)
    if n_t > 1:
        hist = [jnp.where(first, hist[0], hist_st_ref[...])]
        ct_prev = [[jnp.where(first, ct_prev[0][h], ct_st_ref[h]) for h in range(MLSTM_HEADS)]]
        n_prev = [[jnp.where(first, n_prev[0][h], n_st_ref[h:h + 1, :]) for h in range(MLSTM_HEADS)]]
        m_prev = [jnp.where(first, m_prev[0], m_st_ref[...])]

    res = {}
    x = x_ref[...].reshape(nseq * tt, D_MODEL)
    mixer = _mixer_pieces(res, x, t, hist, ct_prev, n_prev, m_prev, wts, scr, nseq=nseq, tt=tt, pos0=pos0,
                          in_ln=in_ln, alpha=alpha)
    if fuse_ffn:
        wg_ref, wu_ref, wd_ref, ln2g_ref, ln2b_ref = ffn_w
        slot = lax.rem(i, jnp.int32(2))
        n_first = N_FF_CHUNKS - N_FF_TAIL_CHUNKS
        tail = _ffn_chunks(hb_ref[...], wg_ref, wu_ref, wd_ref, acc_ref, range(n_first, N_FF_CHUNKS))
        next(tail)
        n_heads = (mp // LANES) * MLSTM_HEADS
        n_body = 2 * n_first
        body = None
        done = 0

        def body_upto(n):
            nonlocal done
            while done < min(n, n_body):
                next(body)
                done += 1

        n_proj = 0
        head = 0
        for label in mixer:
            if label == "ln":
                for _ in range(2 * N_FF_TAIL_CHUNKS - 2):
                    next(tail)
            elif label == "gates":
                next(tail)
            elif label == "proj":
                n_proj += 1
                if n_proj == 1:
                    out_ref[0] = _layer_norm(alpha * h1_ref[slot] + acc_ref[...], ln2g_ref[...], ln2b_ref[...])
                elif n_proj == 5:
                    hb = h1_ref[1 - slot].astype(jnp.bfloat16)
                    hb_ref[...] = hb
                    body = _ffn_chunks(hb, wg_ref, wu_ref, wd_ref, acc_ref, range(n_first))
            elif label == "head":
                head += 1
                body_upto(-(-(n_body - 4) * head // n_heads))
            elif label == "outproj":
                body_upto(done + 2)
        assert n_proj == 5 and done == n_body
    else:
        for _ in mixer:
            pass
    h1, hist_new, ct_new, n_new, m_new = res["h1"], res["hist"], res["ct"], res["n"], res["m"]
    r0 = 0
    for part in h1:
        if fuse_ffn:
            h1_ref[slot, r0:r0 + part.shape[0], :] = part
        elif nseq == 1:
            out_ref[0, r0:r0 + part.shape[0], :] = part
        else:
            out_ref[r0 // tt:(r0 + part.shape[0]) // tt] = part.reshape(part.shape[0] // tt, tt, D_MODEL)
        r0 += part.shape[0]
    if fuse_ffn:
        hist_new = [jnp.where(live, a, b) for a, b in zip(hist_new, hist)]
        ct_new = [[jnp.where(live, a, b) for a, b in zip(cn, cp)] for cn, cp in zip(ct_new, ct_prev)]
        n_new = [[jnp.where(live, a, b) for a, b in zip(nn, no)] for nn, no in zip(n_new, n_prev)]
        m_new = [jnp.where(live, a, b) for a, b in zip(m_new, m_prev)]
    if n_t > 1:
        hist_st_ref[...] = hist_new[0]
        for h in range(MLSTM_HEADS):
            ct_st_ref[h] = ct_new[0][h]
            n_st_ref[h:h + 1, :] = n_new[0][h]
        m_st_ref[...] = m_new[0]

    def write_state():
        for s in range(nseq):
            pool_out_ref[s] = hist_new[s][HIST_ROWS - POOL_HIST:, :]
            for h in range(MLSTM_HEADS):
                c_out_ref[s, h] = ct_new[s][h].T
                n_out_ref[s, h] = n_new[s][h]
                m_out_ref[s, h] = m_new[s][h:h + 1, 0:1]

    if n_t > 1:
        pl.when(t == n_t - 1)(write_state)
    else:
        write_state()


def _const_spec(shape):
    zeros = (0,) * len(shape)
    return pl.BlockSpec(shape, lambda *_: zeros, pipeline_mode=pl.Buffered(1))


def _layer_tiles(batch, seq_len):
    if seq_len >= TILE_ROWS:
        assert seq_len % TILE_ROWS == 0
        return 1, TILE_ROWS
    assert seq_len >= HIST_ROWS and seq_len % SUBLANES == 0
    nseq = max(1, TILE_ROWS // (-(-seq_len // LANES) * LANES))
    while batch % nseq:
        nseq -= 1
    return nseq, seq_len


def _layer(x, state, wts, ffn_w, *, pos0, in_ln, alpha):
    batch, seq_len, _ = x.shape
    nseq, tt = _layer_tiles(batch, seq_len)
    n_t = seq_len // tt
    n_tiles = (batch // nseq) * n_t
    md = nseq * tt
    mp = nseq * (-(-tt // LANES) * LANES)
    fuse_ffn = ffn_w is not None
    zero_init = state is None
    f32, bf16 = jnp.float32, jnp.bfloat16
    kern = functools.partial(_layer_kernel, nseq=nseq, tt=tt, n_t=n_t, n_tiles=n_tiles, pos0=pos0, in_ln=in_ln,
                             alpha=alpha, fuse_ffn=fuse_ffn, zero_init=zero_init)

    def mix_tile(i):
        return jnp.minimum(i, n_tiles - 1)

    def out_tile(i):
        return jnp.maximum(i - 2, 0) if fuse_ffn else i

    row3 = lambda i: (mix_tile(i) // n_t, 0, 0)
    row4 = lambda i: (mix_tile(i) // n_t, 0, 0, 0)
    state_specs = [
        pl.BlockSpec((nseq, POOL_HIST, POOL_WIDTH), row3),
        pl.BlockSpec((nseq, MLSTM_HEADS, HEAD_DIM, HEAD_DIM), row4),
        pl.BlockSpec((nseq, MLSTM_HEADS, 1, HEAD_DIM), row4),
        pl.BlockSpec((nseq, MLSTM_HEADS, 1, 1), row4),
    ]
    state_shapes = [
        jax.ShapeDtypeStruct((batch, POOL_HIST, POOL_WIDTH), f32),
        jax.ShapeDtypeStruct((batch, MLSTM_HEADS, HEAD_DIM, HEAD_DIM), f32),
        jax.ShapeDtypeStruct((batch, MLSTM_HEADS, 1, HEAD_DIM), f32),
        jax.ShapeDtypeStruct((batch, MLSTM_HEADS, 1, 1), f32),
    ]
    state_in = ()
    if not zero_init:
        pool, c, n, m = state
        state_in = (pool, c, n.reshape(batch, MLSTM_HEADS, 1, HEAD_DIM), m.reshape(batch, MLSTM_HEADS, 1, 1))
    weights = wts + (ffn_w if fuse_ffn else ())
    in_specs = ([pl.BlockSpec((nseq, tt, D_MODEL), lambda i: (mix_tile(i) // n_t, mix_tile(i) % n_t, 0))]
                + (state_specs if not zero_init else []) + [_const_spec(w.shape) for w in weights])
    out_specs = [pl.BlockSpec((nseq, tt, D_MODEL), lambda i: (out_tile(i) // n_t, out_tile(i) % n_t, 0))
                 ] + state_specs
    out_shape = [jax.ShapeDtypeStruct((batch, seq_len, D_MODEL), f32)] + state_shapes
    scratch = [
        pltpu.VMEM((md, D_MODEL), f32),
        pltpu.VMEM((nseq, HIST_ROWS + tt, POOL_WIDTH), f32),
        pltpu.VMEM((MLSTM_WIDTH, mp), bf16),
        pltpu.VMEM((mp, MLSTM_WIDTH), f32),
        pltpu.VMEM((mp, MLSTM_WIDTH), bf16),
        pltpu.VMEM((MLSTM_WIDTH, mp), bf16),
        pltpu.VMEM((MLSTM_WIDTH, mp), f32),
        pltpu.VMEM((md, D_MODEL), bf16),
        pltpu.VMEM((SUBLANES, mp), f32),
        pltpu.VMEM((4, SUBLANES, mp), f32),
        pltpu.VMEM((mp, SUBLANES), f32),
        pltpu.VMEM((HIST_ROWS, POOL_WIDTH), f32),
        pltpu.VMEM((MLSTM_HEADS, HEAD_DIM, HEAD_DIM), f32),
        pltpu.VMEM((SUBLANES, HEAD_DIM), f32),
        pltpu.VMEM((SUBLANES, LANES), f32),
    ]
    if fuse_ffn:
        scratch += [pltpu.VMEM((2, md, D_MODEL), f32),
                    pltpu.VMEM((md, D_MODEL), bf16),
                    pltpu.VMEM((md, D_MODEL), f32)]
    out, pool_o, c_o, n_o, m_o = pl.pallas_call(
        kern,
        grid=(n_tiles + (2 if fuse_ffn else 0),),
        in_specs=in_specs,
        out_specs=out_specs,
        out_shape=out_shape,
        scratch_shapes=scratch,
        compiler_params=pltpu.CompilerParams(
            dimension_semantics=("arbitrary",), vmem_limit_bytes=VMEM_LIMIT_BYTES),
        name="layer" if fuse_ffn else "mixer",
    )(x, *state_in, *weights)
    return out, (pool_o, c_o, n_o.reshape(batch, MLSTM_HEADS, HEAD_DIM), m_o.reshape(batch, MLSTM_HEADS))


def _ffn_kernel(h_ref, wg_ref, wu_ref, wd_ref, g_ref, b_ref, y_ref, acc_ref, *, alpha):
    h = h_ref[...]
    for _ in _ffn_chunks(h.astype(jnp.bfloat16), wg_ref, wu_ref, wd_ref, acc_ref, range(N_FF_CHUNKS)):
        pass
    y_ref[...] = _layer_norm(alpha * h + acc_ref[...], g_ref[...], b_ref[...])


def _ffn(h, wg, wu, wd, g, b, *, alpha):
    n = h.shape[0]
    tm = min(n, TILE_ROWS)
    assert n % tm == 0
    row_spec = pl.BlockSpec((tm, D_MODEL), lambda i: (i, 0))
    return pl.pallas_call(
        functools.partial(_ffn_kernel, alpha=alpha),
        grid=(n // tm,),
        in_specs=[row_spec] + [_const_spec(w.shape) for w in (wg, wu, wd, g, b)],
        out_specs=row_spec,
        out_shape=jax.ShapeDtypeStruct((n, D_MODEL), jnp.float32),
        scratch_shapes=[pltpu.VMEM((tm, D_MODEL), jnp.float32)],
        compiler_params=pltpu.CompilerParams(
            dimension_semantics=("arbitrary",), vmem_limit_bytes=VMEM_LIMIT_BYTES),
        name="ffn",
    )(h, wg, wu, wd, g, b)


def _mixer_weights(ln_in_g, ln_in_b, w_in, b_in, w_pool, pool_scale, mlstm_norm_g, w_out, ln1_g, ln1_b):
    f32, bf16 = jnp.float32, jnp.bfloat16
    wd = MLSTM_WIDTH
    pad = 2 * SUBLANES - 2 * MLSTM_HEADS
    w_t = jnp.concatenate([w_in[:, wd:2 * wd], w_in[:, 3 * wd:]], axis=1).T
    b_t = jnp.concatenate([b_in[wd:2 * wd], b_in[3 * wd:]])
    return (
        ln_in_g.reshape(1, -1), ln_in_b.reshape(1, -1),
        jnp.concatenate([w_in[:, :wd], w_in[:, 2 * wd:3 * wd]], axis=1).astype(bf16),
        jnp.concatenate([b_in[:wd], b_in[2 * wd:3 * wd]]).reshape(1, -1).astype(f32),
        jnp.pad(w_t, ((0, pad), (0, 0))).astype(bf16), jnp.pad(b_t, (0, pad)).reshape(-1, 1).astype(f32),
        w_pool.astype(bf16), pool_scale.reshape(1, -1),
        jnp.broadcast_to(mlstm_norm_g.reshape(-1, 1), (MLSTM_WIDTH, LANES)).astype(f32), w_out.astype(bf16),
        ln1_g.reshape(1, -1), ln1_b.reshape(1, -1),
    )


def kernel(x_prompt, x_sample, state_pool, state_mlstm_C, state_mlstm_n, state_mlstm_m, ln_in_g, ln_in_b, w_in, b_in, w_pool, pool_scale, mlstm_norm_g, w_out, ln1_g, ln1_b, w_gate, w_up, w_down, ln2_g, ln2_b):
    depth = w_in.shape[0]
    alpha = (2.0 * depth) ** 0.25
    bf16 = jnp.bfloat16
    hp, hs = x_prompt, x_sample
    acc_p, acc_s = [], []
    for l in range(depth):
        wts = _mixer_weights(ln_in_g, ln_in_b, w_in[l], b_in[l], w_pool[l], pool_scale[l], mlstm_norm_g[l],
                             w_out[l], ln1_g[l], ln1_b[l])
        ffn_w = (w_gate[l].astype(bf16), w_up[l].astype(bf16), w_down[l].astype(bf16),
                 ln2_g[l].reshape(1, -1), ln2_b[l].reshape(1, -1))
        hp, st_p = _layer(hp, None, wts, ffn_w, pos0=0, in_ln=(l == 0), alpha=alpha)
        acc_p.append(st_p)
        sample_state = (state_pool[l], state_mlstm_C[l], state_mlstm_n[l], state_mlstm_m[l])
        h1, st_s = _layer(hs, sample_state, wts, None, pos0=PAST_LEN, in_ln=(l == 0), alpha=alpha)
        hs = _ffn(h1.reshape(-1, D_MODEL), *ffn_w, alpha=alpha).reshape(hs.shape)
        acc_s.append(st_s)
    pool_p, c_p, n_p, m_p = [jnp.stack(a) for a in zip(*acc_p)]
    pool_s, c_s, n_s, m_s = [jnp.stack(a) for a in zip(*acc_s)]
    return (hp, hs, pool_p, c_p, n_p, m_p, pool_s, c_s, n_s, m_s)
```

```python
import functools

import jax
import jax.numpy as jnp
from jax import lax
from jax.experimental import pallas as pl
from jax.experimental.pallas import tpu as pltpu

D_MODEL = 1024
POOL_WIDTH = 512
POOL_WINDOWS = (2, 4, 8, 16)
POOL_GROUP_WIDTH = 128
POOL_HIST = 15
HIST_ROWS = 16
MLSTM_WIDTH = 512
MLSTM_HEADS = 4
HEAD_DIM = 128
D_FF = 2816
FF_CHUNK = 256
LN_EPS = 1e-5
PAST_LEN = 1024

LANES = 128
SUBLANES = 8
TILE_ROWS = 512
VMEM_LIMIT_BYTES = 60 * 1024 * 1024

_NT = (((1,), (1,)), ((), ()))


def _layer_norm(x, g, b):
    mu = jnp.mean(x, axis=-1, keepdims=True)
    xc = x - mu
    var = jnp.mean(xc * xc, axis=-1, keepdims=True)
    return xc * lax.rsqrt(var + LN_EPS) * g + b


def _log_sigmoid(x):
    return -(jnp.maximum(-x, 0.0) + jnp.log1p(jnp.exp(-jnp.abs(x))))


def _dot(a, b):
    return jnp.dot(a, b, preferred_element_type=jnp.float32)


N_FF_CHUNKS = D_FF // FF_CHUNK
N_FF_TAIL_CHUNKS = 2


def _ffn_chunks(hb, wg_ref, wu_ref, wd_ref, acc_ref, chunks):
    for ci in chunks:
        cs = slice(ci * FF_CHUNK, (ci + 1) * FF_CHUNK)
        gate = _dot(hb, wg_ref[:, cs])
        up = _dot(hb, wu_ref[:, cs])
        yield
        part = _dot((jax.nn.silu(gate) * up).astype(jnp.bfloat16), wd_ref[cs, :])
        if ci == 0:
            acc_ref[...] = part
        else:
            acc_ref[...] += part
        yield


def _mixer_pieces(res, x, t, hist, c_prev, m_prev, wts, scr, *, nseq, tt, pos0, in_ln, alpha):
    (lng_ref, lnb_ref, w_in_ref, b_in_ref, wkT_ref, bkT_ref, wgT_ref, bgT_ref, w_pool_ref, pscale_ref, ng_ref,
     w_out_ref, ln1g_ref, ln1b_ref) = wts
    hp_ref, ext_ref, q_ref, kT_ref, v_ref, o_ref, mix_ref, c_ref, pt_ref = scr
    f32 = jnp.float32
    bf16 = jnp.bfloat16
    md = nseq * tt
    pp = -(-tt // LANES) * LANES
    mp = nseq * pp
    cps = pp // LANES

    hp = _layer_norm(x, lng_ref[...], lnb_ref[...]) if in_ln else x
    hp_ref[...] = hp
    hp_bf = hp.astype(bf16)
    if pp > tt:
        zpad = jnp.zeros((pp - tt, D_MODEL), bf16)
        hp_bf = jnp.concatenate([a for s in range(nseq) for a in (hp_bf[s * tt:(s + 1) * tt], zpad)], axis=0)
    yield "ln"

    def dense_rows(a):
        if pp == tt:
            return a
        return jnp.concatenate([a[s * pp:s * pp + tt] for s in range(nseq)], axis=0)

    gT = lax.dot_general(wgT_ref[...], hp_bf, _NT, preferred_element_type=f32) + bgT_ref[...]

    lane = lax.broadcasted_iota(jnp.int32, (SUBLANES, LANES), 1)
    sub = lax.broadcasted_iota(jnp.int32, (SUBLANES, LANES), 0)
    n_chunks = mp // LANES
    for j in range(n_chunks):
        g8 = gT[:, j * LANES:(j + 1) * LANES]
        b8 = _log_sigmoid(g8)
        k = 1
        while k < LANES:
            b8 = b8 + jnp.where(lane >= k, pltpu.roll(b8, k, 1), 0.0)
            k *= 2
        c8 = g8 - pltpu.roll(b8, MLSTM_HEADS, 0)
        cm8 = c8
        k = 1
        while k < LANES:
            cm8 = jnp.maximum(cm8, jnp.where(lane >= k, pltpu.roll(cm8, k, 1), -jnp.inf))
            k *= 2
        c_ref[:, j * LANES:(j + 1) * LANES] = c8
        pt_ref[j * LANES:(j + 1) * LANES, :] = jnp.where(sub < MLSTM_HEADS, cm8, b8).T
    yield "gates"

    row = lax.broadcasted_iota(jnp.int32, (tt, 1), 0)
    n_avail = pos0 + t * tt + row + 1

    def pool_group(g):
        w = POOL_WINDOWS[g]
        gs = slice(g * POOL_GROUP_WIDTH, (g + 1) * POOL_GROUP_WIDTH)
        inv_cnt = 1.0 / jnp.minimum(n_avail, w).astype(f32)
        ds = []
        for s in range(nseq):
            tok = ext_ref[s, HIST_ROWS:HIST_ROWS + tt, gs]
            acc = tok
            for sh in range(1, w):
                acc = acc + ext_ref[s, HIST_ROWS - sh:HIST_ROWS - sh + tt, gs]
            ds.append(acc * inv_cnt - tok)
        d = ds[0] if nseq == 1 else jnp.concatenate(ds, axis=0)
        p = _dot(d.astype(bf16), w_pool_ref[g]) * pscale_ref[:, gs]
        mix_ref[:, gs] = p.astype(bf16)

    u = _dot(hp_bf, w_in_ref[:, 0:512]) + b_in_ref[:, 0:512]
    hist_new = []
    for s in range(nseq):
        ext_ref[s, 0:HIST_ROWS, :] = hist[s]
        ext_ref[s, HIST_ROWS:HIST_ROWS + tt, :] = u[s * pp:s * pp + tt]
        hist_new.append(ext_ref[s, tt:tt + HIST_ROWS, :])
    yield "proj"
    q = _dot(hp_bf, w_in_ref[:, 512:1024]) + b_in_ref[:, 512:1024]
    q_ref[...] = q.astype(bf16)
    pool_group(0)
    pool_group(1)
    yield "proj"
    v = _dot(hp_bf, w_in_ref[:, 1024:1536]) + b_in_ref[:, 1024:1536]
    for h in range(MLSTM_HEADS):
        v_ref[:, h * 256:h * 256 + HEAD_DIM] = v[:, h * HEAD_DIM:(h + 1) * HEAD_DIM].astype(bf16)
    pool_group(2)
    yield "proj"
    o = _dot(hp_bf, w_in_ref[:, 1536:2048]) + b_in_ref[:, 1536:2048]
    o_ref[...] = dense_rows(o)
    pool_group(3)
    yield "proj"
    kT = lax.dot_general(wkT_ref[...], hp_bf, _NT, preferred_element_type=f32) + bkT_ref[...]
    kT_ref[...] = kT * (HEAD_DIM ** -0.5)
    yield "proj"

    tril = (lax.broadcasted_iota(jnp.int32, (LANES, LANES), 0)
            >= lax.broadcasted_iota(jnp.int32, (LANES, LANES), 1))
    lane_row = lax.broadcasted_iota(jnp.int32, (1, LANES), 1)
    m_state = [list(m) for m in m_prev]
    c_state = [list(c) for c in c_prev]
    for j in range(n_chunks):
        s, jj = divmod(j, cps)
        rows = slice(j * LANES, (j + 1) * LANES)
        valid = min(LANES, tt - jj * LANES)
        d0 = s * tt + jj * LANES
        for h in range(MLSTM_HEADS):
            hs = slice(h * HEAD_DIM, (h + 1) * HEAD_DIM)
            qh = q_ref[rows, hs]
            kTh = kT_ref[hs, rows]
            vaug = v_ref[rows, h * 256:(h + 1) * 256]
            c_row = c_ref[h:h + 1, rows]
            cmax = jnp.broadcast_to(pt_ref[rows, h:h + 1], (LANES, LANES))
            bcum = jnp.broadcast_to(pt_ref[rows, MLSTM_HEADS + h:MLSTM_HEADS + h + 1], (LANES, LANES))
            m_row = m_state[s][h]
            c_aug = c_state[s][h]

            gmax = jnp.maximum(m_row, cmax)
            dmat = jnp.where(tril, jnp.exp(c_row - gmax), 0.0)
            sc = _dot(qh, kTh.astype(bf16)) * dmat
            sv = _dot(sc.astype(bf16), vaug)
            qc = _dot(qh, c_aug.astype(bf16))
            dec = jnp.exp(m_row - gmax)
            num = dec * qc[:, :HEAD_DIM] + sv[:, :HEAD_DIM]
            den = dec * qc[:, HEAD_DIM:] + sv[:, HEAD_DIM:]
            hh = num / jnp.maximum(jnp.abs(den), jnp.exp(-(bcum + gmax)))

            mu = jnp.mean(hh, axis=-1, keepdims=True)
            hc = hh - mu
            var = jnp.mean(hc * hc, axis=-1, keepdims=True)
            hn = hc * lax.rsqrt(var + LN_EPS)
            out = hn[:valid] * ng_ref[:, hs] * jax.nn.sigmoid(o_ref[d0:d0 + valid, hs])
            mix_ref[d0:d0 + valid, POOL_WIDTH + h * HEAD_DIM:POOL_WIDTH + (h + 1) * HEAD_DIM] = out.astype(bf16)

            g_last = gmax[valid - 1:valid, :]
            m_new = bcum[valid - 1:valid, :] + g_last
            w_state = jnp.exp(m_row - g_last)
            w_row = jnp.exp(c_row - g_last)
            if valid < LANES:
                w_row = jnp.where(lane_row < valid, w_row, 0.0)
            kv = _dot((kTh * w_row).astype(bf16), vaug)
            c_state[s][h] = jnp.concatenate([w_state, w_state], axis=1) * c_aug + kv
            m_state[s][h] = m_new
            yield "head"

    half = md // 2
    if half % SUBLANES or (nseq > 1 and half % tt):
        half = md
    h1_parts = []
    for r0 in range(0, md, half):
        mixo = _dot(mix_ref[r0:r0 + half, :], w_out_ref[...])
        yield "outproj"
        h1_parts.append(_layer_norm(alpha * hp_ref[r0:r0 + half, :] + mixo, ln1g_ref[...], ln1b_ref[...]))
    res["h1"] = h1_parts
    res["hist"] = hist_new
    res["c"] = c_state
    res["m"] = m_state
    yield "ln1"


N_MIXER_WEIGHTS = 14
N_FFN_WEIGHTS = 5
N_MIXER_SCRATCH = 9


def _layer_kernel(*refs, nseq, tt, n_t, n_tiles, pos0, in_ln, alpha, fuse_ffn, zero_init):
    assert nseq == 1 or (n_t == 1 and not fuse_ffn)
    x_ref = refs[0]
    pos = 1
    if not zero_init:
        pool0_ref, c0_ref, n0_ref, m0_ref = refs[pos:pos + 4]
        pos += 4
    wts = refs[pos:pos + N_MIXER_WEIGHTS]
    pos += N_MIXER_WEIGHTS
    if fuse_ffn:
        ffn_w = refs[pos:pos + N_FFN_WEIGHTS]
        pos += N_FFN_WEIGHTS
    out_ref, pool_out_ref, c_out_ref, n_out_ref, m_out_ref = refs[pos:pos + 5]
    pos += 5
    scr = refs[pos:pos + N_MIXER_SCRATCH]
    pos += N_MIXER_SCRATCH
    hist_st_ref, c_st_ref, m_st_ref = refs[pos:pos + 3]
    pos += 3
    if fuse_ffn:
        h1_ref, hb_ref, acc_ref = refs[pos:pos + 3]

    f32 = jnp.float32
    mp = nseq * (-(-tt // LANES) * LANES)
    i = pl.program_id(0)
    live = i < n_tiles
    t = lax.rem(jnp.minimum(i, n_tiles - 1), jnp.int32(n_t))
    first = t == 0
    v_ref = scr[4]

    @pl.when(i == 0)
    def _init():
        for h in range(MLSTM_HEADS):
            v_ref[:, h * 256 + HEAD_DIM:(h + 1) * 256] = jnp.ones((mp, HEAD_DIM), jnp.bfloat16)
        hist_st_ref[...] = jnp.zeros_like(hist_st_ref)
        c_st_ref[...] = jnp.zeros_like(c_st_ref)
        m_st_ref[...] = jnp.zeros_like(m_st_ref)
        if fuse_ffn:
            h1_ref[...] = jnp.zeros_like(h1_ref)
            hb_ref[...] = jnp.zeros_like(hb_ref)
            acc_ref[...] = jnp.zeros_like(acc_ref)

    hist, c_prev, m_prev = [], [], []
    for s in range(nseq):
        if zero_init:
            hist.append(jnp.zeros((HIST_ROWS, POOL_WIDTH), f32))
            c_prev.append([jnp.zeros((HEAD_DIM, 2 * HEAD_DIM), f32)] * MLSTM_HEADS)
            m_prev.append([jnp.zeros((1, LANES), f32)] * MLSTM_HEADS)
        else:
            hist.append(jnp.concatenate(
                [jnp.zeros((HIST_ROWS - POOL_HIST, POOL_WIDTH), f32), pool0_ref[s]], axis=0))
            c_prev.append([jnp.concatenate(
                [c0_ref[s, h], jnp.broadcast_to(n0_ref[s, h], (HEAD_DIM, HEAD_DIM)).T], axis=1)
                for h in range(MLSTM_HEADS)])
            m_prev.append([jnp.broadcast_to(m0_ref[s, h], (1, LANES)) for h in range(MLSTM_HEADS)])
    if n_t > 1:
        hist = [jnp.where(first, hist[0], hist_st_ref[...])]
        c_prev = [[jnp.where(first, c_prev[0][h], c_st_ref[h]) for h in range(MLSTM_HEADS)]]
        m_prev = [[jnp.where(first, m_prev[0][h], m_st_ref[h:h + 1, :]) for h in range(MLSTM_HEADS)]]

    res = {}
    x = x_ref[...].reshape(nseq * tt, D_MODEL)
    mixer = _mixer_pieces(res, x, t, hist, c_prev, m_prev, wts, scr, nseq=nseq, tt=tt, pos0=pos0,
                          in_ln=in_ln, alpha=alpha)
    if fuse_ffn:
        wg_ref, wu_ref, wd_ref, ln2g_ref, ln2b_ref = ffn_w
        slot = lax.rem(i, jnp.int32(2))
        n_first = N_FF_CHUNKS - N_FF_TAIL_CHUNKS
        tail = _ffn_chunks(hb_ref[...], wg_ref, wu_ref, wd_ref, acc_ref, range(n_first, N_FF_CHUNKS))
        next(tail)
        n_heads = (mp // LANES) * MLSTM_HEADS
        n_body = 2 * n_first
        body = None
        done = 0

        def body_upto(n):
            nonlocal done
            while done < min(n, n_body):
                next(body)
                done += 1

        n_proj = 0
        head = 0
        for label in mixer:
            if label == "ln":
                for _ in range(2 * N_FF_TAIL_CHUNKS - 2):
                    next(tail)
            elif label == "gates":
                next(tail)
            elif label == "proj":
                n_proj += 1
                if n_proj == 1:
                    out_ref[0] = _layer_norm(alpha * h1_ref[slot] + acc_ref[...], ln2g_ref[...], ln2b_ref[...])
                elif n_proj == 5:
                    hb = h1_ref[1 - slot].astype(jnp.bfloat16)
                    hb_ref[...] = hb
                    body = _ffn_chunks(hb, wg_ref, wu_ref, wd_ref, acc_ref, range(n_first))
            elif label == "head":
                head += 1
                body_upto(-(-(n_body - 4) * head // n_heads))
            elif label == "outproj":
                body_upto(done + 2)
        assert n_proj == 5 and done == n_body
    else:
        for _ in mixer:
            pass
    h1, hist_new, c_new, m_new = res["h1"], res["hist"], res["c"], res["m"]
    r0 = 0
    for part in h1:
        if fuse_ffn:
            h1_ref[slot, r0:r0 + part.shape[0], :] = part
        elif nseq == 1:
            out_ref[0, r0:r0 + part.shape[0], :] = part
        else:
            out_ref[r0 // tt:(r0 + part.shape[0]) // tt] = part.reshape(part.shape[0] // tt, tt, D_MODEL)
        r0 += part.shape[0]
    if fuse_ffn:
        hist_new = [jnp.where(live, a, b) for a, b in zip(hist_new, hist)]
        c_new = [[jnp.where(live, a, b) for a, b in zip(cn, cp)] for cn, cp in zip(c_new, c_prev)]
        m_new = [[jnp.where(live, a, b) for a, b in zip(mn, mo)] for mn, mo in zip(m_new, m_prev)]
    if n_t > 1:
        hist_st_ref[...] = hist_new[0]
        for h in range(MLSTM_HEADS):
            c_st_ref[h] = c_new[0][h]
            m_st_ref[h:h + 1, :] = m_new[0][h]

    def write_state():
        for s in range(nseq):
            pool_out_ref[s] = hist_new[s][HIST_ROWS - POOL_HIST:, :]
            for h in range(MLSTM_HEADS):
                c_out_ref[s, h] = c_new[s][h][:, :HEAD_DIM]
                n_out_ref[s, h] = c_new[s][h][:, HEAD_DIM:].T[0:1, :]
                m_out_ref[s, h] = m_new[s][h][:, 0:1]

    if n_t > 1:
        pl.when(t == n_t - 1)(write_state)
    else:
        write_state()


def _const_spec(shape):
    zeros = (0,) * len(shape)
    return pl.BlockSpec(shape, lambda *_: zeros, pipeline_mode=pl.Buffered(1))


def _layer_tiles(batch, seq_len):
    if seq_len >= TILE_ROWS:
        assert seq_len % TILE_ROWS == 0
        return 1, TILE_ROWS
    assert seq_len >= HIST_ROWS and seq_len % SUBLANES == 0
    nseq = max(1, TILE_ROWS // (-(-seq_len // LANES) * LANES))
    while batch % nseq:
        nseq -= 1
    return nseq, seq_len


def _layer(x, state, wts, ffn_w, *, pos0, in_ln, alpha):
    batch, seq_len, _ = x.shape
    nseq, tt = _layer_tiles(batch, seq_len)
    n_t = seq_len // tt
    n_tiles = (batch // nseq) * n_t
    md = nseq * tt
    mp = nseq * (-(-tt // LANES) * LANES)
    fuse_ffn = ffn_w is not None
    zero_init = state is None
    f32, bf16 = jnp.float32, jnp.bfloat16
    kern = functools.partial(_layer_kernel, nseq=nseq, tt=tt, n_t=n_t, n_tiles=n_tiles, pos0=pos0, in_ln=in_ln,
                             alpha=alpha, fuse_ffn=fuse_ffn, zero_init=zero_init)

    def mix_tile(i):
        return jnp.minimum(i, n_tiles - 1)

    def out_tile(i):
        return jnp.maximum(i - 2, 0) if fuse_ffn else i

    row3 = lambda i: (mix_tile(i) // n_t, 0, 0)
    row4 = lambda i: (mix_tile(i) // n_t, 0, 0, 0)
    state_specs = [
        pl.BlockSpec((nseq, POOL_HIST, POOL_WIDTH), row3),
        pl.BlockSpec((nseq, MLSTM_HEADS, HEAD_DIM, HEAD_DIM), row4),
        pl.BlockSpec((nseq, MLSTM_HEADS, 1, HEAD_DIM), row4),
        pl.BlockSpec((nseq, MLSTM_HEADS, 1, 1), row4),
    ]
    state_shapes = [
        jax.ShapeDtypeStruct((batch, POOL_HIST, POOL_WIDTH), f32),
        jax.ShapeDtypeStruct((batch, MLSTM_HEADS, HEAD_DIM, HEAD_DIM), f32),
        jax.ShapeDtypeStruct((batch, MLSTM_HEADS, 1, HEAD_DIM), f32),
        jax.ShapeDtypeStruct((batch, MLSTM_HEADS, 1, 1), f32),
    ]
    state_in = ()
    if not zero_init:
        pool, c, n, m = state
        state_in = (pool, c, n.reshape(batch, MLSTM_HEADS, 1, HEAD_DIM), m.reshape(batch, MLSTM_HEADS, 1, 1))
    weights = wts + (ffn_w if fuse_ffn else ())
    in_specs = ([pl.BlockSpec((nseq, tt, D_MODEL), lambda i: (mix_tile(i) // n_t, mix_tile(i) % n_t, 0))]
                + (state_specs if not zero_init else []) + [_const_spec(w.shape) for w in weights])
    out_specs = [pl.BlockSpec((nseq, tt, D_MODEL), lambda i: (out_tile(i) // n_t, out_tile(i) % n_t, 0))
                 ] + state_specs
    out_shape = [jax.ShapeDtypeStruct((batch, seq_len, D_MODEL), f32)] + state_shapes
    scratch = [
        pltpu.VMEM((md, D_MODEL), f32),
        pltpu.VMEM((nseq, HIST_ROWS + tt, POOL_WIDTH), f32),
        pltpu.VMEM((mp, MLSTM_WIDTH), bf16),
        pltpu.VMEM((MLSTM_WIDTH, mp), f32),
        pltpu.VMEM((mp, 2 * MLSTM_WIDTH), bf16),
        pltpu.VMEM((md, MLSTM_WIDTH), f32),
        pltpu.VMEM((md, D_MODEL), bf16),
        pltpu.VMEM((SUBLANES, mp), f32),
        pltpu.VMEM((mp, SUBLANES), f32),
        pltpu.VMEM((HIST_ROWS, POOL_WIDTH), f32),
        pltpu.VMEM((MLSTM_HEADS, HEAD_DIM, 2 * HEAD_DIM), f32),
        pltpu.VMEM((SUBLANES, LANES), f32),
    ]
    if fuse_ffn:
        scratch += [pltpu.VMEM((2, md, D_MODEL), f32),
                    pltpu.VMEM((md, D_MODEL), bf16),
                    pltpu.VMEM((md, D_MODEL), f32)]
    out, pool_o, c_o, n_o, m_o = pl.pallas_call(
        kern,
        grid=(n_tiles + (2 if fuse_ffn else 0),),
        in_specs=in_specs,
        out_specs=out_specs,
        out_shape=out_shape,
        scratch_shapes=scratch,
        compiler_params=pltpu.CompilerParams(
            dimension_semantics=("arbitrary",), vmem_limit_bytes=VMEM_LIMIT_BYTES),
        name="layer" if fuse_ffn else "mixer",
    )(x, *state_in, *weights)
    return out, (pool_o, c_o, n_o.reshape(batch, MLSTM_HEADS, HEAD_DIM), m_o.reshape(batch, MLSTM_HEADS))


def _ffn_kernel(h_ref, wg_ref, wu_ref, wd_ref, g_ref, b_ref, y_ref, acc_ref, *, alpha):
    h = h_ref[...]
    for _ in _ffn_chunks(h.astype(jnp.bfloat16), wg_ref, wu_ref, wd_ref, acc_ref, range(N_FF_CHUNKS)):
        pass
    y_ref[...] = _layer_norm(alpha * h + acc_ref[...], g_ref[...], b_ref[...])


def _ffn(h, wg, wu, wd, g, b, *, alpha):
    n = h.shape[0]
    tm = min(n, TILE_ROWS)
    assert n % tm == 0
    row_spec = pl.BlockSpec((tm, D_MODEL), lambda i: (i, 0))
    return pl.pallas_call(
        functools.partial(_ffn_kernel, alpha=alpha),
        grid=(n // tm,),
        in_specs=[row_spec] + [_const_spec(w.shape) for w in (wg, wu, wd, g, b)],
        out_specs=row_spec,
        out_shape=jax.ShapeDtypeStruct((n, D_MODEL), jnp.float32),
        scratch_shapes=[pltpu.VMEM((tm, D_MODEL), jnp.float32)],
        compiler_params=pltpu.CompilerParams(
            dimension_semantics=("arbitrary",), vmem_limit_bytes=VMEM_LIMIT_BYTES),
        name="ffn",
    )(h, wg, wu, wd, g, b)


def _mixer_weights(ln_in_g, ln_in_b, w_in, b_in, w_pool, pool_scale, mlstm_norm_g, w_out, ln1_g, ln1_b):
    f32, bf16 = jnp.float32, jnp.bfloat16
    k0, k1, g0 = 2 * 512, 3 * 512, 5 * 512
    w_uqvo = jnp.concatenate([w_in[:, :k0], w_in[:, k1:g0]], axis=1).astype(bf16)
    b_uqvo = jnp.concatenate([b_in[:k0], b_in[k1:g0]]).reshape(1, -1).astype(f32)
    return (
        ln_in_g.reshape(1, -1), ln_in_b.reshape(1, -1), w_uqvo, b_uqvo,
        w_in[:, k0:k1].T.astype(bf16), b_in[k0:k1].reshape(-1, 1).astype(f32),
        w_in[:, g0:].T.astype(bf16), b_in[g0:].reshape(-1, 1).astype(f32),
        w_pool.astype(bf16), pool_scale.reshape(1, -1), mlstm_norm_g.reshape(1, -1), w_out.astype(bf16),
        ln1_g.reshape(1, -1), ln1_b.reshape(1, -1),
    )


def kernel(x_prompt, x_sample, state_pool, state_mlstm_C, state_mlstm_n, state_mlstm_m, ln_in_g, ln_in_b, w_in, b_in, w_pool, pool_scale, mlstm_norm_g, w_out, ln1_g, ln1_b, w_gate, w_up, w_down, ln2_g, ln2_b):
    depth = w_in.shape[0]
    alpha = (2.0 * depth) ** 0.25
    bf16 = jnp.bfloat16
    hp, hs = x_prompt, x_sample
    acc_p, acc_s = [], []
    for l in range(depth):
        wts = _mixer_weights(ln_in_g, ln_in_b, w_in[l], b_in[l], w_pool[l], pool_scale[l], mlstm_norm_g[l],
                             w_out[l], ln1_g[l], ln1_b[l])
        ffn_w = (w_gate[l].astype(bf16), w_up[l].astype(bf16), w_down[l].astype(bf16),
                 ln2_g[l].reshape(1, -1), ln2_b[l].reshape(1, -1))
        hp, st_p = _layer(hp, None, wts, ffn_w, pos0=0, in_ln=(l == 0), alpha=alpha)
        acc_p.append(st_p)
        sample_state = (state_pool[l], state_mlstm_C[l], state_mlstm_n[l], state_mlstm_m[l])
        h1, st_s = _layer(hs, sample_state, wts, None, pos0=PAST_LEN, in_ln=(l == 0), alpha=alpha)
        hs = _ffn(h1.reshape(-1, D_MODEL), *ffn_w, alpha=alpha).reshape(hs.shape)
        acc_s.append(st_s)
    pool_p, c_p, n_p, m_p = [jnp.stack(a) for a in zip(*acc_p)]
    pool_s, c_s, n_s, m_s = [jnp.stack(a) for a in zip(*acc_s)]
    return (hp, hs, pool_p, c_p, n_p, m_p, pool_s, c_s, n_s, m_s)
```

```python
import functools

import jax
import jax.numpy as jnp
from jax import lax
from jax.experimental import pallas as pl
from jax.experimental.pallas import tpu as pltpu

D_MODEL = 1024
POOL_WIDTH = 512
POOL_WINDOWS = (2, 4, 8, 16)
POOL_GROUP_WIDTH = 128
POOL_HIST = 15
HIST_ROWS = 16
MLSTM_WIDTH = 512
MLSTM_HEADS = 4
HEAD_DIM = 128
D_FF = 2816
FF_CHUNK = 256
LN_EPS = 1e-5
PAST_LEN = 1024

LANES = 128
SUBLANES = 8
TILE_ROWS = 512
VMEM_LIMIT_BYTES = 60 * 1024 * 1024

_NT = (((1,), (1,)), ((), ()))


def _layer_norm(x, g, b):
    mu = jnp.mean(x, axis=-1, keepdims=True)
    xc = x - mu
    var = jnp.mean(xc * xc, axis=-1, keepdims=True)
    return xc * lax.rsqrt(var + LN_EPS) * g + b


def _log_sigmoid(x):
    return -(jnp.maximum(-x, 0.0) + jnp.log1p(jnp.exp(-jnp.abs(x))))


def _dot(a, b):
    return jnp.dot(a, b, preferred_element_type=jnp.float32)


N_FF_CHUNKS = D_FF // FF_CHUNK


def _ffn_up_chunks(hb, wg_ref, wu_ref, act_ref):
    for ci in range(N_FF_CHUNKS):
        cs = slice(ci * FF_CHUNK, (ci + 1) * FF_CHUNK)
        act_ref[:, cs] = (jax.nn.silu(_dot(hb, wg_ref[:, cs])) * _dot(hb, wu_ref[:, cs])).astype(jnp.bfloat16)
        yield


def _mixer_pieces(res, x, t, hist, c_prev, m_prev, wts, scr, *, nseq, tt, pos0, in_ln, alpha):
    (lng_ref, lnb_ref, w_in_ref, b_in_ref, wkT_ref, bkT_ref, wgT_ref, bgT_ref, w_pool_ref, pscale_ref, ng_ref,
     w_out_ref, ln1g_ref, ln1b_ref) = wts
    hp_ref, ext_ref, q_ref, kT_ref, v_ref, o_ref, mix_ref, c_ref, pt_ref = scr
    f32 = jnp.float32
    bf16 = jnp.bfloat16
    md = nseq * tt
    pp = -(-tt // LANES) * LANES
    mp = nseq * pp
    cps = pp // LANES

    hp = _layer_norm(x, lng_ref[...], lnb_ref[...]) if in_ln else x
    hp_ref[...] = hp
    hp_bf = hp.astype(bf16)
    if pp > tt:
        zpad = jnp.zeros((pp - tt, D_MODEL), bf16)
        hp_bf = jnp.concatenate([a for s in range(nseq) for a in (hp_bf[s * tt:(s + 1) * tt], zpad)], axis=0)
    yield "ln"

    def dense_rows(a):
        if pp == tt:
            return a
        return jnp.concatenate([a[s * pp:s * pp + tt] for s in range(nseq)], axis=0)

    gT = lax.dot_general(wgT_ref[...], hp_bf, _NT, preferred_element_type=f32) + bgT_ref[...]

    lane = lax.broadcasted_iota(jnp.int32, (SUBLANES, LANES), 1)
    sub = lax.broadcasted_iota(jnp.int32, (SUBLANES, LANES), 0)
    n_chunks = mp // LANES
    for j in range(n_chunks):
        g8 = gT[:, j * LANES:(j + 1) * LANES]
        b8 = _log_sigmoid(g8)
        k = 1
        while k < LANES:
            b8 = b8 + jnp.where(lane >= k, pltpu.roll(b8, k, 1), 0.0)
            k *= 2
        c8 = g8 - pltpu.roll(b8, MLSTM_HEADS, 0)
        cm8 = c8
        k = 1
        while k < LANES:
            cm8 = jnp.maximum(cm8, jnp.where(lane >= k, pltpu.roll(cm8, k, 1), -jnp.inf))
            k *= 2
        c_ref[:, j * LANES:(j + 1) * LANES] = c8
        pt_ref[j * LANES:(j + 1) * LANES, :] = jnp.where(sub < MLSTM_HEADS, cm8, b8).T
    yield "gates"

    row = lax.broadcasted_iota(jnp.int32, (tt, 1), 0)
    n_avail = pos0 + t * tt + row + 1

    def pool_group(g):
        w = POOL_WINDOWS[g]
        gs = slice(g * POOL_GROUP_WIDTH, (g + 1) * POOL_GROUP_WIDTH)
        inv_cnt = 1.0 / jnp.minimum(n_avail, w).astype(f32)
        ds = []
        for s in range(nseq):
            tok = ext_ref[s, HIST_ROWS:HIST_ROWS + tt, gs]
            acc = tok
            for sh in range(1, w):
                acc = acc + ext_ref[s, HIST_ROWS - sh:HIST_ROWS - sh + tt, gs]
            ds.append(acc * inv_cnt - tok)
        d = ds[0] if nseq == 1 else jnp.concatenate(ds, axis=0)
        p = _dot(d.astype(bf16), w_pool_ref[g]) * pscale_ref[:, gs]
        mix_ref[:, gs] = p.astype(bf16)

    u = _dot(hp_bf, w_in_ref[:, 0:512]) + b_in_ref[:, 0:512]
    hist_new = []
    for s in range(nseq):
        ext_ref[s, 0:HIST_ROWS, :] = hist[s]
        ext_ref[s, HIST_ROWS:HIST_ROWS + tt, :] = u[s * pp:s * pp + tt]
        hist_new.append(ext_ref[s, tt:tt + HIST_ROWS, :])
    yield "proj"
    q = _dot(hp_bf, w_in_ref[:, 512:1024]) + b_in_ref[:, 512:1024]
    q_ref[...] = q.astype(bf16)
    pool_group(0)
    pool_group(1)
    yield "proj"
    v = _dot(hp_bf, w_in_ref[:, 1024:1536]) + b_in_ref[:, 1024:1536]
    for h in range(MLSTM_HEADS):
        v_ref[:, h * 256:h * 256 + HEAD_DIM] = v[:, h * HEAD_DIM:(h + 1) * HEAD_DIM].astype(bf16)
    pool_group(2)
    yield "proj"
    o = _dot(hp_bf, w_in_ref[:, 1536:2048]) + b_in_ref[:, 1536:2048]
    o_ref[...] = dense_rows(o)
    pool_group(3)
    yield "proj"
    kT = lax.dot_general(wkT_ref[...], hp_bf, _NT, preferred_element_type=f32) + bkT_ref[...]
    kT_ref[...] = kT * (HEAD_DIM ** -0.5)
    yield "proj"

    tril = (lax.broadcasted_iota(jnp.int32, (LANES, LANES), 0)
            >= lax.broadcasted_iota(jnp.int32, (LANES, LANES), 1))
    lane_row = lax.broadcasted_iota(jnp.int32, (1, LANES), 1)
    m_state = [list(m) for m in m_prev]
    c_state = [list(c) for c in c_prev]
    for j in range(n_chunks):
        s, jj = divmod(j, cps)
        rows = slice(j * LANES, (j + 1) * LANES)
        valid = min(LANES, tt - jj * LANES)
        d0 = s * tt + jj * LANES
        for h in range(MLSTM_HEADS):
            hs = slice(h * HEAD_DIM, (h + 1) * HEAD_DIM)
            qh = q_ref[rows, hs]
            kTh = kT_ref[hs, rows]
            vaug = v_ref[rows, h * 256:(h + 1) * 256]
            c_row = c_ref[h:h + 1, rows]
            cmax = jnp.broadcast_to(pt_ref[rows, h:h + 1], (LANES, LANES))
            bcum = jnp.broadcast_to(pt_ref[rows, MLSTM_HEADS + h:MLSTM_HEADS + h + 1], (LANES, LANES))
            m_row = m_state[s][h]
            c_aug = c_state[s][h]

            gmax = jnp.maximum(m_row, cmax)
            dmat = jnp.where(tril, jnp.exp(c_row - gmax), 0.0)
            sc = _dot(qh, kTh.astype(bf16)) * dmat
            sv = _dot(sc.astype(bf16), vaug)
            qc = _dot(qh, c_aug.astype(bf16))
            dec = jnp.exp(m_row - gmax)
            num = dec * qc[:, :HEAD_DIM] + sv[:, :HEAD_DIM]
            den = dec * qc[:, HEAD_DIM:] + sv[:, HEAD_DIM:]
            hh = num / jnp.maximum(jnp.abs(den), jnp.exp(-(bcum + gmax)))

            mu = jnp.mean(hh, axis=-1, keepdims=True)
            hc = hh - mu
            var = jnp.mean(hc * hc, axis=-1, keepdims=True)
            hn = hc * lax.rsqrt(var + LN_EPS)
            out = hn[:valid] * ng_ref[:, hs] * jax.nn.sigmoid(o_ref[d0:d0 + valid, hs])
            mix_ref[d0:d0 + valid, POOL_WIDTH + h * HEAD_DIM:POOL_WIDTH + (h + 1) * HEAD_DIM] = out.astype(bf16)

            g_last = gmax[valid - 1:valid, :]
            m_new = bcum[valid - 1:valid, :] + g_last
            w_state = jnp.exp(m_row - g_last)
            w_row = jnp.exp(c_row - g_last)
            if valid < LANES:
                w_row = jnp.where(lane_row < valid, w_row, 0.0)
            kv = _dot((kTh * w_row).astype(bf16), vaug)
            c_state[s][h] = jnp.concatenate([w_state, w_state], axis=1) * c_aug + kv
            m_state[s][h] = m_new
            yield "head"

    half = md // 2
    if half % SUBLANES or (nseq > 1 and half % tt):
        half = md
    h1_parts = []
    for r0 in range(0, md, half):
        mixo = _dot(mix_ref[r0:r0 + half, :], w_out_ref[...])
        yield "outproj"
        h1_parts.append(_layer_norm(alpha * hp_ref[r0:r0 + half, :] + mixo, ln1g_ref[...], ln1b_ref[...]))
    res["h1"] = h1_parts
    res["hist"] = hist_new
    res["c"] = c_state
    res["m"] = m_state
    yield "ln1"


N_MIXER_WEIGHTS = 14
N_FFN_WEIGHTS = 5
N_MIXER_SCRATCH = 9


def _layer_kernel(*refs, nseq, tt, n_t, n_tiles, pos0, in_ln, alpha, fuse_ffn, zero_init):
    assert nseq == 1 or (n_t == 1 and not fuse_ffn)
    x_ref = refs[0]
    pos = 1
    if not zero_init:
        pool0_ref, c0_ref, n0_ref, m0_ref = refs[pos:pos + 4]
        pos += 4
    wts = refs[pos:pos + N_MIXER_WEIGHTS]
    pos += N_MIXER_WEIGHTS
    if fuse_ffn:
        ffn_w = refs[pos:pos + N_FFN_WEIGHTS]
        pos += N_FFN_WEIGHTS
    out_ref, pool_out_ref, c_out_ref, n_out_ref, m_out_ref = refs[pos:pos + 5]
    pos += 5
    scr = refs[pos:pos + N_MIXER_SCRATCH]
    pos += N_MIXER_SCRATCH
    hist_st_ref, c_st_ref, m_st_ref = refs[pos:pos + 3]
    pos += 3
    if fuse_ffn:
        h1_ref, act_ref = refs[pos:pos + 2]

    f32 = jnp.float32
    mp = nseq * (-(-tt // LANES) * LANES)
    i = pl.program_id(0)
    live = i < n_tiles
    t = lax.rem(jnp.minimum(i, n_tiles - 1), jnp.int32(n_t))
    first = t == 0
    v_ref = scr[4]

    @pl.when(i == 0)
    def _init():
        for h in range(MLSTM_HEADS):
            v_ref[:, h * 256 + HEAD_DIM:(h + 1) * 256] = jnp.ones((mp, HEAD_DIM), jnp.bfloat16)
        hist_st_ref[...] = jnp.zeros_like(hist_st_ref)
        c_st_ref[...] = jnp.zeros_like(c_st_ref)
        m_st_ref[...] = jnp.zeros_like(m_st_ref)
        if fuse_ffn:
            h1_ref[...] = jnp.zeros_like(h1_ref)
            act_ref[...] = jnp.zeros_like(act_ref)

    hist, c_prev, m_prev = [], [], []
    for s in range(nseq):
        if zero_init:
            hist.append(jnp.zeros((HIST_ROWS, POOL_WIDTH), f32))
            c_prev.append([jnp.zeros((HEAD_DIM, 2 * HEAD_DIM), f32)] * MLSTM_HEADS)
            m_prev.append([jnp.zeros((1, LANES), f32)] * MLSTM_HEADS)
        else:
            hist.append(jnp.concatenate(
                [jnp.zeros((HIST_ROWS - POOL_HIST, POOL_WIDTH), f32), pool0_ref[s]], axis=0))
            c_prev.append([jnp.concatenate(
                [c0_ref[s, h], jnp.broadcast_to(n0_ref[s, h], (HEAD_DIM, HEAD_DIM))], axis=1)
                for h in range(MLSTM_HEADS)])
            m_prev.append([jnp.broadcast_to(m0_ref[s, h], (1, LANES)) for h in range(MLSTM_HEADS)])
    if n_t > 1:
        hist = [jnp.where(first, hist[0], hist_st_ref[...])]
        c_prev = [[jnp.where(first, c_prev[0][h], c_st_ref[h]) for h in range(MLSTM_HEADS)]]
        m_prev = [[jnp.where(first, m_prev[0][h], m_st_ref[h:h + 1, :]) for h in range(MLSTM_HEADS)]]

    res = {}
    x = x_ref[...].reshape(nseq * tt, D_MODEL)
    mixer = _mixer_pieces(res, x, t, hist, c_prev, m_prev, wts, scr, nseq=nseq, tt=tt, pos0=pos0,
                          in_ln=in_ln, alpha=alpha)
    if fuse_ffn:
        wg_ref, wu_ref, wd_ref, ln2g_ref, ln2b_ref = ffn_w
        slot = lax.rem(i, jnp.int32(2))
        down = _dot(act_ref[...], wd_ref[...])
        n_heads = (mp // LANES) * MLSTM_HEADS
        body = None
        done = 0

        def body_upto(n):
            nonlocal done
            while done < min(n, N_FF_CHUNKS):
                next(body)
                done += 1

        n_proj = 0
        head = 0
        for label in mixer:
            if label == "proj":
                n_proj += 1
                if n_proj == 1:
                    out_ref[0] = _layer_norm(alpha * h1_ref[slot] + down, ln2g_ref[...], ln2b_ref[...])
                elif n_proj == 5:
                    body = _ffn_up_chunks(h1_ref[1 - slot].astype(jnp.bfloat16), wg_ref, wu_ref, act_ref)
            elif label == "head":
                head += 1
                body_upto(-(-(N_FF_CHUNKS - 2) * head // n_heads))
            elif label == "outproj":
                body_upto(done + 1)
        assert n_proj == 5 and done == N_FF_CHUNKS
    else:
        for _ in mixer:
            pass
    h1, hist_new, c_new, m_new = res["h1"], res["hist"], res["c"], res["m"]
    r0 = 0
    for part in h1:
        if fuse_ffn:
            h1_ref[slot, r0:r0 + part.shape[0], :] = part
        elif nseq == 1:
            out_ref[0, r0:r0 + part.shape[0], :] = part
        else:
            out_ref[r0 // tt:(r0 + part.shape[0]) // tt] = part.reshape(part.shape[0] // tt, tt, D_MODEL)
        r0 += part.shape[0]
    if fuse_ffn:
        hist_new = [jnp.where(live, a, b) for a, b in zip(hist_new, hist)]
        c_new = [[jnp.where(live, a, b) for a, b in zip(cn, cp)] for cn, cp in zip(c_new, c_prev)]
        m_new = [[jnp.where(live, a, b) for a, b in zip(mn, mo)] for mn, mo in zip(m_new, m_prev)]
    if n_t > 1:
        hist_st_ref[...] = hist_new[0]
        for h in range(MLSTM_HEADS):
            c_st_ref[h] = c_new[0][h]
            m_st_ref[h:h + 1, :] = m_new[0][h]
    for s in range(nseq):
        pool_out_ref[s] = hist_new[s][HIST_ROWS - POOL_HIST:, :]
        for h in range(MLSTM_HEADS):
            c_out_ref[s, h] = c_new[s][h][:, :HEAD_DIM]
            n_out_ref[s, h] = c_new[s][h][:, HEAD_DIM:HEAD_DIM + 1]
            m_out_ref[s, h] = m_new[s][h][:, 0:1]


def _const_spec(shape):
    zeros = (0,) * len(shape)
    return pl.BlockSpec(shape, lambda *_: zeros, pipeline_mode=pl.Buffered(1))


def _layer_tiles(batch, seq_len):
    if seq_len >= TILE_ROWS:
        assert seq_len % TILE_ROWS == 0
        return 1, TILE_ROWS
    assert seq_len >= HIST_ROWS and seq_len % SUBLANES == 0
    nseq = max(1, TILE_ROWS // (-(-seq_len // LANES) * LANES))
    while batch % nseq:
        nseq -= 1
    return nseq, seq_len


def _layer(x, state, wts, ffn_w, *, pos0, in_ln, alpha):
    batch, seq_len, _ = x.shape
    nseq, tt = _layer_tiles(batch, seq_len)
    n_t = seq_len // tt
    n_tiles = (batch // nseq) * n_t
    md = nseq * tt
    mp = nseq * (-(-tt // LANES) * LANES)
    fuse_ffn = ffn_w is not None
    zero_init = state is None
    f32, bf16 = jnp.float32, jnp.bfloat16
    kern = functools.partial(_layer_kernel, nseq=nseq, tt=tt, n_t=n_t, n_tiles=n_tiles, pos0=pos0, in_ln=in_ln,
                             alpha=alpha, fuse_ffn=fuse_ffn, zero_init=zero_init)

    def mix_tile(i):
        return jnp.minimum(i, n_tiles - 1)

    def out_tile(i):
        return jnp.maximum(i - 2, 0) if fuse_ffn else i

    row3 = lambda i: (mix_tile(i) // n_t, 0, 0)
    row4 = lambda i: (mix_tile(i) // n_t, 0, 0, 0)
    state_specs = [
        pl.BlockSpec((nseq, POOL_HIST, POOL_WIDTH), row3),
        pl.BlockSpec((nseq, MLSTM_HEADS, HEAD_DIM, HEAD_DIM), row4),
        pl.BlockSpec((nseq, MLSTM_HEADS, HEAD_DIM, 1), row4),
        pl.BlockSpec((nseq, MLSTM_HEADS, 1, 1), row4),
    ]
    state_shapes = [
        jax.ShapeDtypeStruct((batch, POOL_HIST, POOL_WIDTH), f32),
        jax.ShapeDtypeStruct((batch, MLSTM_HEADS, HEAD_DIM, HEAD_DIM), f32),
        jax.ShapeDtypeStruct((batch, MLSTM_HEADS, HEAD_DIM, 1), f32),
        jax.ShapeDtypeStruct((batch, MLSTM_HEADS, 1, 1), f32),
    ]
    state_in = ()
    if not zero_init:
        pool, c, n, m = state
        state_in = (pool, c, n.reshape(batch, MLSTM_HEADS, HEAD_DIM, 1), m.reshape(batch, MLSTM_HEADS, 1, 1))
    weights = wts + (ffn_w if fuse_ffn else ())
    in_specs = ([pl.BlockSpec((nseq, tt, D_MODEL), lambda i: (mix_tile(i) // n_t, mix_tile(i) % n_t, 0))]
                + (state_specs if not zero_init else []) + [_const_spec(w.shape) for w in weights])
    out_specs = [pl.BlockSpec((nseq, tt, D_MODEL), lambda i: (out_tile(i) // n_t, out_tile(i) % n_t, 0))
                 ] + state_specs
    out_shape = [jax.ShapeDtypeStruct((batch, seq_len, D_MODEL), f32)] + state_shapes
    scratch = [
        pltpu.VMEM((md, D_MODEL), f32),
        pltpu.VMEM((nseq, HIST_ROWS + tt, POOL_WIDTH), f32),
        pltpu.VMEM((mp, MLSTM_WIDTH), bf16),
        pltpu.VMEM((MLSTM_WIDTH, mp), f32),
        pltpu.VMEM((mp, 2 * MLSTM_WIDTH), bf16),
        pltpu.VMEM((md, MLSTM_WIDTH), f32),
        pltpu.VMEM((md, D_MODEL), bf16),
        pltpu.VMEM((SUBLANES, mp), f32),
        pltpu.VMEM((mp, SUBLANES), f32),
        pltpu.VMEM((HIST_ROWS, POOL_WIDTH), f32),
        pltpu.VMEM((MLSTM_HEADS, HEAD_DIM, 2 * HEAD_DIM), f32),
        pltpu.VMEM((SUBLANES, LANES), f32),
    ]
    if fuse_ffn:
        scratch += [pltpu.VMEM((2, md, D_MODEL), f32),
                    pltpu.VMEM((md, D_FF), bf16)]
    out, pool_o, c_o, n_o, m_o = pl.pallas_call(
        kern,
        grid=(n_tiles + (2 if fuse_ffn else 0),),
        in_specs=in_specs,
        out_specs=out_specs,
        out_shape=out_shape,
        scratch_shapes=scratch,
        compiler_params=pltpu.CompilerParams(
            dimension_semantics=("arbitrary",), vmem_limit_bytes=VMEM_LIMIT_BYTES),
        name="layer" if fuse_ffn else "mixer",
    )(x, *state_in, *weights)
    return out, (pool_o, c_o, n_o.reshape(batch, MLSTM_HEADS, HEAD_DIM), m_o.reshape(batch, MLSTM_HEADS))


def _ffn_kernel(h_ref, wg_ref, wu_ref, wd_ref, g_ref, b_ref, y_ref, act_ref, *, alpha):
    h = h_ref[...]
    for _ in _ffn_up_chunks(h.astype(jnp.bfloat16), wg_ref, wu_ref, act_ref):
        pass
    y_ref[...] = _layer_norm(alpha * h + _dot(act_ref[...], wd_ref[...]), g_ref[...], b_ref[...])


def _ffn(h, wg, wu, wd, g, b, *, alpha):
    n = h.shape[0]
    tm = min(n, TILE_ROWS)
    assert n % tm == 0
    row_spec = pl.BlockSpec((tm, D_MODEL), lambda i: (i, 0))
    return pl.pallas_call(
        functools.partial(_ffn_kernel, alpha=alpha),
        grid=(n // tm,),
        in_specs=[row_spec] + [_const_spec(w.shape) for w in (wg, wu, wd, g, b)],
        out_specs=row_spec,
        out_shape=jax.ShapeDtypeStruct((n, D_MODEL), jnp.float32),
        scratch_shapes=[pltpu.VMEM((tm, D_FF), jnp.bfloat16)],
        compiler_params=pltpu.CompilerParams(
            dimension_semantics=("arbitrary",), vmem_limit_bytes=VMEM_LIMIT_BYTES),
        name="ffn",
    )(h, wg, wu, wd, g, b)


def _mixer_weights(ln_in_g, ln_in_b, w_in, b_in, w_pool, pool_scale, mlstm_norm_g, w_out, ln1_g, ln1_b):
    f32, bf16 = jnp.float32, jnp.bfloat16
    k0, k1, g0 = 2 * 512, 3 * 512, 5 * 512
    w_uqvo = jnp.concatenate([w_in[:, :k0], w_in[:, k1:g0]], axis=1).astype(bf16)
    b_uqvo = jnp.concatenate([b_in[:k0], b_in[k1:g0]]).reshape(1, -1).astype(f32)
    return (
        ln_in_g.reshape(1, -1), ln_in_b.reshape(1, -1), w_uqvo, b_uqvo,
        w_in[:, k0:k1].T.astype(bf16), b_in[k0:k1].reshape(-1, 1).astype(f32),
        w_in[:, g0:].T.astype(bf16), b_in[g0:].reshape(-1, 1).astype(f32),
        w_pool.astype(bf16), pool_scale.reshape(1, -1), mlstm_norm_g.reshape(1, -1), w_out.astype(bf16),
        ln1_g.reshape(1, -1), ln1_b.reshape(1, -1),
    )


def kernel(x_prompt, x_sample, state_pool, state_mlstm_C, state_mlstm_n, state_mlstm_m, ln_in_g, ln_in_b, w_in, b_in, w_pool, pool_scale, mlstm_norm_g, w_out, ln1_g, ln1_b, w_gate, w_up, w_down, ln2_g, ln2_b):
    depth = w_in.shape[0]
    alpha = (2.0 * depth) ** 0.25
    bf16 = jnp.bfloat16
    hp, hs = x_prompt, x_sample
    acc_p, acc_s = [], []
    for l in range(depth):
        wts = _mixer_weights(ln_in_g, ln_in_b, w_in[l], b_in[l], w_pool[l], pool_scale[l], mlstm_norm_g[l],
                             w_out[l], ln1_g[l], ln1_b[l])
        ffn_w = (w_gate[l].astype(bf16), w_up[l].astype(bf16), w_down[l].astype(bf16),
                 ln2_g[l].reshape(1, -1), ln2_b[l].reshape(1, -1))
        hp, st_p = _layer(hp, None, wts, ffn_w, pos0=0, in_ln=(l == 0), alpha=alpha)
        acc_p.append(st_p)
        sample_state = (state_pool[l], state_mlstm_C[l], state_mlstm_n[l], state_mlstm_m[l])
        h1, st_s = _layer(hs, sample_state, wts, None, pos0=PAST_LEN, in_ln=(l == 0), alpha=alpha)
        hs = _ffn(h1.reshape(-1, D_MODEL), *ffn_w, alpha=alpha).reshape(hs.shape)
        acc_s.append(st_s)
    pool_p, c_p, n_p, m_p = [jnp.stack(a) for a in zip(*acc_p)]
    pool_s, c_s, n_s, m_s = [jnp.stack(a) for a in zip(*acc_s)]
    return (hp, hs, pool_p, c_p, n_p, m_p, pool_s, c_s, n_s, m_s)
```

```python
import functools

import jax
import jax.numpy as jnp
from jax import lax
from jax.experimental import pallas as pl
from jax.experimental.pallas import tpu as pltpu

D_MODEL = 1024
POOL_WIDTH = 512
POOL_WINDOWS = (2, 4, 8, 16)
POOL_GROUP_WIDTH = 128
POOL_HIST = 15
HIST_ROWS = 16
MLSTM_WIDTH = 512
MLSTM_HEADS = 4
HEAD_DIM = 128
VAUG = 2 * HEAD_DIM
D_FF = 2816
FF_CHUNK = 256
LN_EPS = 1e-5
PAST_LEN = 1024

LANES = 128
SUBLANES = 8
TILE_ROWS = 512
VMEM_LIMIT_BYTES = 60 * 1024 * 1024

_NT = (((1,), (1,)), ((), ()))


def _layer_norm(x, g, b):
    mu = jnp.mean(x, axis=-1, keepdims=True)
    xc = x - mu
    var = jnp.mean(xc * xc, axis=-1, keepdims=True)
    return xc * lax.rsqrt(var + LN_EPS) * g + b


def _log_sigmoid(x):
    return -(jnp.maximum(-x, 0.0) + jnp.log1p(jnp.exp(-jnp.abs(x))))


def _dot(a, b):
    return jnp.dot(a, b, preferred_element_type=jnp.float32)


N_FF_CHUNKS = D_FF // FF_CHUNK
LN_ROW_PARTS = 2


def _ffn_up_chunks(hb, wg_ref, wu_ref, act_ref):
    for ci in range(N_FF_CHUNKS):
        cs = slice(ci * FF_CHUNK, (ci + 1) * FF_CHUNK)
        act_ref[:, cs] = (jax.nn.silu(_dot(hb, wg_ref[:, cs])) * _dot(hb, wu_ref[:, cs])).astype(jnp.bfloat16)
        yield


def _mixer_pieces(res, x, t, hist, c_prev, m_prev, wts, scr, *, nseq, tt, pos0, in_ln, alpha):
    (lng_ref, lnb_ref, w_in_ref, b_in_ref, wkT_ref, bkT_ref, wgT_ref, bgT_ref, w_pool_ref, pscale_ref, ng_ref,
     w_out_ref, ln1g_ref, ln1b_ref) = wts
    hp_ref, ext_ref, q_ref, kT_ref, v_ref, o_ref, mix_ref, c_ref, pt_ref = scr
    f32 = jnp.float32
    bf16 = jnp.bfloat16
    md = nseq * tt
    pp = -(-tt // LANES) * LANES
    mp = nseq * pp
    cps = pp // LANES

    lane = lax.broadcasted_iota(jnp.int32, (SUBLANES, LANES), 1)
    sub = lax.broadcasted_iota(jnp.int32, (SUBLANES, LANES), 0)
    n_chunks = mp // LANES

    def gate_scans(g8, j):
        b8 = _log_sigmoid(g8)
        k = 1
        while k < LANES:
            b8 = b8 + jnp.where(lane >= k, pltpu.roll(b8, k, 1), 0.0)
            k *= 2
        c8 = g8 - pltpu.roll(b8, MLSTM_HEADS, 0)
        cm8 = c8
        k = 1
        while k < LANES:
            cm8 = jnp.maximum(cm8, jnp.where(lane >= k, pltpu.roll(cm8, k, 1), -jnp.inf))
            k *= 2
        c_ref[:, j * LANES:(j + 1) * LANES] = c8
        pt_ref[j * LANES:(j + 1) * LANES, :] = jnp.where(sub < MLSTM_HEADS, cm8, b8).T

    if pp == tt and md % (LN_ROW_PARTS * LANES) == 0:
        qr = md // LN_ROW_PARTS
    elif pp > tt:
        qr = tt
    else:
        qr = md
    pc = qr if pp == tt else pp
    hp_parts = []
    for r in range(md // qr):
        hp = _layer_norm(x[r * qr:(r + 1) * qr], lng_ref[...], lnb_ref[...]) if in_ln else x[r * qr:(r + 1) * qr]
        hp_ref[r * qr:(r + 1) * qr, :] = hp
        hb = hp.astype(bf16)
        hp_parts.append(hb)
        if pc > qr:
            hb = jnp.concatenate([hb, jnp.zeros((pc - qr, D_MODEL), bf16)], axis=0)
        cols = slice(r * pc, (r + 1) * pc)
        gT = lax.dot_general(wgT_ref[...], hb, _NT, preferred_element_type=f32) + bgT_ref[...]
        for jj in range(pc // LANES):
            gate_scans(gT[:, jj * LANES:(jj + 1) * LANES], r * (pc // LANES) + jj)
        kT = lax.dot_general(wkT_ref[...], hb, _NT, preferred_element_type=f32) + bkT_ref[...]
        kT_ref[:, cols] = kT * (HEAD_DIM ** -0.5)
        yield "ln_part"
    hp_bf = hp_parts[0] if len(hp_parts) == 1 else jnp.concatenate(hp_parts, axis=0)
    if pp > tt:
        zpad = jnp.zeros((pp - tt, D_MODEL), bf16)
        hp_bf = jnp.concatenate([a for s in range(nseq) for a in (hp_bf[s * tt:(s + 1) * tt], zpad)], axis=0)

    def dense_rows(a):
        if pp == tt:
            return a
        return jnp.concatenate([a[s * pp:s * pp + tt] for s in range(nseq)], axis=0)

    row = lax.broadcasted_iota(jnp.int32, (tt, 1), 0)
    n_avail = pos0 + t * tt + row + 1

    def pool_group(g):
        w = POOL_WINDOWS[g]
        gs = slice(g * POOL_GROUP_WIDTH, (g + 1) * POOL_GROUP_WIDTH)
        inv_cnt = 1.0 / jnp.minimum(n_avail, w).astype(f32)
        ds = []
        for s in range(nseq):
            tok = ext_ref[s, HIST_ROWS:HIST_ROWS + tt, gs]
            acc = tok
            for sh in range(1, w):
                acc = acc + ext_ref[s, HIST_ROWS - sh:HIST_ROWS - sh + tt, gs]
            ds.append(acc * inv_cnt - tok)
        d = ds[0] if nseq == 1 else jnp.concatenate(ds, axis=0)
        p = _dot(d.astype(bf16), w_pool_ref[g]) * pscale_ref[:, gs]
        mix_ref[:, gs] = p.astype(bf16)

    u = _dot(hp_bf, w_in_ref[:, 0:512]) + b_in_ref[:, 0:512]
    hist_new = []
    for s in range(nseq):
        ext_ref[s, 0:HIST_ROWS, :] = hist[s]
        ext_ref[s, HIST_ROWS:HIST_ROWS + tt, :] = u[s * pp:s * pp + tt]
        hist_new.append(ext_ref[s, tt:tt + HIST_ROWS, :])
    yield "proj"
    q = _dot(hp_bf, w_in_ref[:, 512:1024]) + b_in_ref[:, 512:1024]
    q_ref[...] = q.astype(bf16)
    pool_group(0)
    pool_group(1)
    yield "proj"
    v = _dot(hp_bf, w_in_ref[:, 1024:1536]) + b_in_ref[:, 1024:1536]
    for h in range(MLSTM_HEADS):
        v_ref[:, h * VAUG:h * VAUG + HEAD_DIM] = v[:, h * HEAD_DIM:(h + 1) * HEAD_DIM].astype(bf16)
    pool_group(2)
    yield "proj"
    o = _dot(hp_bf, w_in_ref[:, 1536:2048]) + b_in_ref[:, 1536:2048]
    o_ref[...] = dense_rows(o)
    pool_group(3)
    yield "proj_done"

    tril = (lax.broadcasted_iota(jnp.int32, (LANES, LANES), 0)
            >= lax.broadcasted_iota(jnp.int32, (LANES, LANES), 1))
    lane_row = lax.broadcasted_iota(jnp.int32, (1, LANES), 1)
    m_state = [list(m) for m in m_prev]
    c_state = [list(c) for c in c_prev]
    for j in range(n_chunks):
        s, jj = divmod(j, cps)
        rows = slice(j * LANES, (j + 1) * LANES)
        valid = min(LANES, tt - jj * LANES)
        d0 = s * tt + jj * LANES
        for h in range(MLSTM_HEADS):
            hs = slice(h * HEAD_DIM, (h + 1) * HEAD_DIM)
            qh = q_ref[rows, hs]
            kTh = kT_ref[hs, rows]
            vaug = v_ref[rows, h * VAUG:(h + 1) * VAUG]
            c_row = c_ref[h:h + 1, rows]
            cmax = jnp.broadcast_to(pt_ref[rows, h:h + 1], (LANES, LANES))
            bcum = jnp.broadcast_to(pt_ref[rows, MLSTM_HEADS + h:MLSTM_HEADS + h + 1], (LANES, LANES))
            m_row = m_state[s][h]
            c_aug = c_state[s][h]

            gmax = jnp.maximum(m_row, cmax)
            dmat = jnp.where(tril, jnp.exp(c_row - gmax), 0.0)
            sc = _dot(qh, kTh.astype(bf16)) * dmat
            sv = _dot(sc.astype(bf16), vaug)
            qc = _dot(qh, c_aug.astype(bf16))
            dec = jnp.exp(m_row - gmax)
            num = dec * qc[:, :HEAD_DIM] + sv[:, :HEAD_DIM]
            den = dec * qc[:, HEAD_DIM:] + sv[:, HEAD_DIM:]
            hh = num / jnp.maximum(jnp.abs(den), jnp.exp(-(bcum + gmax)))

            mu = jnp.mean(hh, axis=-1, keepdims=True)
            hc = hh - mu
            var = jnp.mean(hc * hc, axis=-1, keepdims=True)
            hn = hc * lax.rsqrt(var + LN_EPS)
            out = hn[:valid] * ng_ref[:, hs] * jax.nn.sigmoid(o_ref[d0:d0 + valid, hs])
            mix_ref[d0:d0 + valid, POOL_WIDTH + h * HEAD_DIM:POOL_WIDTH + (h + 1) * HEAD_DIM] = out.astype(bf16)

            g_last = gmax[valid - 1:valid, :]
            m_new = bcum[valid - 1:valid, :] + g_last
            w_state = jnp.exp(m_row - g_last)
            w_row = jnp.exp(c_row - g_last)
            if valid < LANES:
                w_row = jnp.where(lane_row < valid, w_row, 0.0)
            kv = _dot((kTh * w_row).astype(bf16), vaug)
            c_state[s][h] = jnp.concatenate([w_state, w_state], axis=1) * c_aug + kv
            m_state[s][h] = m_new
            yield "head"

    half = md // 2
    if half % SUBLANES or (nseq > 1 and half % tt):
        half = md
    h1_parts = []
    for r0 in range(0, md, half):
        mixo = _dot(mix_ref[r0:r0 + half, :], w_out_ref[...])
        yield "outproj"
        h1_parts.append(_layer_norm(alpha * hp_ref[r0:r0 + half, :] + mixo, ln1g_ref[...], ln1b_ref[...]))
    res["h1"] = h1_parts
    res["hist"] = hist_new
    res["c"] = c_state
    res["m"] = m_state
    yield "ln1"


N_MIXER_WEIGHTS = 14
N_FFN_WEIGHTS = 5
N_MIXER_SCRATCH = 9


def _layer_kernel(*refs, nseq, tt, n_t, n_tiles, pos0, in_ln, alpha, fuse_ffn, zero_init):
    assert nseq == 1 or (n_t == 1 and not fuse_ffn)
    x_ref = refs[0]
    pos = 1
    if not zero_init:
        pool0_ref, c0_ref, n0_ref, m0_ref = refs[pos:pos + 4]
        pos += 4
    wts = refs[pos:pos + N_MIXER_WEIGHTS]
    pos += N_MIXER_WEIGHTS
    if fuse_ffn:
        ffn_w = refs[pos:pos + N_FFN_WEIGHTS]
        pos += N_FFN_WEIGHTS
    out_ref, pool_out_ref, c_out_ref, n_out_ref, m_out_ref = refs[pos:pos + 5]
    pos += 5
    scr = refs[pos:pos + N_MIXER_SCRATCH]
    pos += N_MIXER_SCRATCH
    hist_st_ref, c_st_ref, m_st_ref = refs[pos:pos + 3]
    pos += 3
    if fuse_ffn:
        h1_ref, act_ref = refs[pos:pos + 2]

    f32 = jnp.float32
    mp = nseq * (-(-tt // LANES) * LANES)
    i = pl.program_id(0)
    live = i < n_tiles
    t = lax.rem(jnp.minimum(i, n_tiles - 1), jnp.int32(n_t))
    first = t == 0
    v_ref = scr[4]

    @pl.when(i == 0)
    def _init():
        for h in range(MLSTM_HEADS):
            v_ref[:, h * VAUG + HEAD_DIM:(h + 1) * VAUG] = jnp.ones((mp, HEAD_DIM), jnp.bfloat16)
        hist_st_ref[...] = jnp.zeros_like(hist_st_ref)
        c_st_ref[...] = jnp.zeros_like(c_st_ref)
        m_st_ref[...] = jnp.zeros_like(m_st_ref)
        if fuse_ffn:
            h1_ref[...] = jnp.zeros_like(h1_ref)
            act_ref[...] = jnp.zeros_like(act_ref)

    hist, c_prev, m_prev = [], [], []
    for s in range(nseq):
        if zero_init:
            hist.append(jnp.zeros((HIST_ROWS, POOL_WIDTH), f32))
            c_prev.append([jnp.zeros((HEAD_DIM, 2 * HEAD_DIM), f32)] * MLSTM_HEADS)
            m_prev.append([jnp.zeros((1, LANES), f32)] * MLSTM_HEADS)
        else:
            hist.append(jnp.concatenate(
                [jnp.zeros((HIST_ROWS - POOL_HIST, POOL_WIDTH), f32), pool0_ref[s]], axis=0))
            c_prev.append([jnp.concatenate(
                [c0_ref[s, h], jnp.broadcast_to(n0_ref[s, h], (HEAD_DIM, HEAD_DIM))], axis=1)
                for h in range(MLSTM_HEADS)])
            m_prev.append([jnp.broadcast_to(m0_ref[s, h], (1, LANES)) for h in range(MLSTM_HEADS)])
    if n_t > 1:
        hist = [jnp.where(first, hist[0], hist_st_ref[...])]
        c_prev = [[jnp.where(first, c_prev[0][h], c_st_ref[h]) for h in range(MLSTM_HEADS)]]
        m_prev = [[jnp.where(first, m_prev[0][h], m_st_ref[h:h + 1, :]) for h in range(MLSTM_HEADS)]]

    res = {}
    x = x_ref[...].reshape(nseq * tt, D_MODEL)
    mixer = _mixer_pieces(res, x, t, hist, c_prev, m_prev, wts, scr, nseq=nseq, tt=tt, pos0=pos0,
                          in_ln=in_ln, alpha=alpha)
    if fuse_ffn:
        wg_ref, wu_ref, wd_ref, ln2g_ref, ln2b_ref = ffn_w
        slot = lax.rem(i, jnp.int32(2))
        dcols = D_MODEL // LN_ROW_PARTS
        down = []

        def down_upto(n):
            while len(down) < n:
                c0 = len(down) * dcols
                down.append(_dot(act_ref[...], wd_ref[:, c0:c0 + dcols]))

        down_upto(1)
        n_heads = (mp // LANES) * MLSTM_HEADS
        body = None
        done = 0

        def body_upto(n):
            nonlocal done
            while done < min(n, N_FF_CHUNKS):
                next(body)
                done += 1

        n_proj = 0
        head = 0
        for label in mixer:
            if label == "ln_part":
                down_upto(min(len(down) + 1, LN_ROW_PARTS))
            elif label == "proj":
                n_proj += 1
                if n_proj == 1:
                    down_upto(LN_ROW_PARTS)
                    ffn_out = jnp.concatenate(down, axis=1)
                    out_ref[0] = _layer_norm(alpha * h1_ref[slot] + ffn_out, ln2g_ref[...], ln2b_ref[...])
            elif label == "proj_done":
                body = _ffn_up_chunks(h1_ref[1 - slot].astype(jnp.bfloat16), wg_ref, wu_ref, act_ref)
            elif label == "head":
                head += 1
                body_upto(-(-(N_FF_CHUNKS - 4) * head // n_heads))
            elif label == "outproj":
                body_upto(done + 2)
        assert n_proj == 3 and done == N_FF_CHUNKS
    else:
        for _ in mixer:
            pass
    h1, hist_new, c_new, m_new = res["h1"], res["hist"], res["c"], res["m"]
    r0 = 0
    for part in h1:
        if fuse_ffn:
            h1_ref[slot, r0:r0 + part.shape[0], :] = part
        elif nseq == 1:
            out_ref[0, r0:r0 + part.shape[0], :] = part
        else:
            out_ref[r0 // tt:(r0 + part.shape[0]) // tt] = part.reshape(part.shape[0] // tt, tt, D_MODEL)
        r0 += part.shape[0]
    if fuse_ffn:
        hist_new = [jnp.where(live, a, b) for a, b in zip(hist_new, hist)]
        c_new = [[jnp.where(live, a, b) for a, b in zip(cn, cp)] for cn, cp in zip(c_new, c_prev)]
        m_new = [[jnp.where(live, a, b) for a, b in zip(mn, mo)] for mn, mo in zip(m_new, m_prev)]
    if n_t > 1:
        hist_st_ref[...] = hist_new[0]
        for h in range(MLSTM_HEADS):
            c_st_ref[h] = c_new[0][h]
            m_st_ref[h:h + 1, :] = m_new[0][h]
    for s in range(nseq):
        pool_out_ref[s] = hist_new[s][HIST_ROWS - POOL_HIST:, :]
        for h in range(MLSTM_HEADS):
            c_out_ref[s, h] = c_new[s][h][:, :HEAD_DIM]
            n_out_ref[s, h] = c_new[s][h][:, HEAD_DIM:HEAD_DIM + 1]
            m_out_ref[s, h] = m_new[s][h][:, 0:1]


def _const_spec(shape):
    zeros = (0,) * len(shape)
    return pl.BlockSpec(shape, lambda *_: zeros, pipeline_mode=pl.Buffered(1))


def _layer_tiles(batch, seq_len):
    if seq_len >= TILE_ROWS:
        assert seq_len % TILE_ROWS == 0
        return 1, TILE_ROWS
    assert seq_len >= HIST_ROWS and seq_len % SUBLANES == 0
    nseq = max(1, TILE_ROWS // (-(-seq_len // LANES) * LANES))
    while batch % nseq:
        nseq -= 1
    return nseq, seq_len


def _layer(x, state, wts, ffn_w, *, pos0, in_ln, alpha):
    batch, seq_len, _ = x.shape
    nseq, tt = _layer_tiles(batch, seq_len)
    n_t = seq_len // tt
    n_tiles = (batch // nseq) * n_t
    md = nseq * tt
    mp = nseq * (-(-tt // LANES) * LANES)
    fuse_ffn = ffn_w is not None
    zero_init = state is None
    f32, bf16 = jnp.float32, jnp.bfloat16
    kern = functools.partial(_layer_kernel, nseq=nseq, tt=tt, n_t=n_t, n_tiles=n_tiles, pos0=pos0, in_ln=in_ln,
                             alpha=alpha, fuse_ffn=fuse_ffn, zero_init=zero_init)

    def mix_tile(i):
        return jnp.minimum(i, n_tiles - 1)

    def out_tile(i):
        return jnp.maximum(i - 2, 0) if fuse_ffn else i

    row3 = lambda i: (mix_tile(i) // n_t, 0, 0)
    row4 = lambda i: (mix_tile(i) // n_t, 0, 0, 0)
    state_specs = [
        pl.BlockSpec((nseq, POOL_HIST, POOL_WIDTH), row3),
        pl.BlockSpec((nseq, MLSTM_HEADS, HEAD_DIM, HEAD_DIM), row4),
        pl.BlockSpec((nseq, MLSTM_HEADS, HEAD_DIM, 1), row4),
        pl.BlockSpec((nseq, MLSTM_HEADS, 1, 1), row4),
    ]
    state_shapes = [
        jax.ShapeDtypeStruct((batch, POOL_HIST, POOL_WIDTH), f32),
        jax.ShapeDtypeStruct((batch, MLSTM_HEADS, HEAD_DIM, HEAD_DIM), f32),
        jax.ShapeDtypeStruct((batch, MLSTM_HEADS, HEAD_DIM, 1), f32),
        jax.ShapeDtypeStruct((batch, MLSTM_HEADS, 1, 1), f32),
    ]
    state_in = ()
    if not zero_init:
        pool, c, n, m = state
        state_in = (pool, c, n.reshape(batch, MLSTM_HEADS, HEAD_DIM, 1), m.reshape(batch, MLSTM_HEADS, 1, 1))
    weights = wts + (ffn_w if fuse_ffn else ())
    in_specs = ([pl.BlockSpec((nseq, tt, D_MODEL), lambda i: (mix_tile(i) // n_t, mix_tile(i) % n_t, 0))]
                + (state_specs if not zero_init else []) + [_const_spec(w.shape) for w in weights])
    out_specs = [pl.BlockSpec((nseq, tt, D_MODEL), lambda i: (out_tile(i) // n_t, out_tile(i) % n_t, 0))
                 ] + state_specs
    out_shape = [jax.ShapeDtypeStruct((batch, seq_len, D_MODEL), f32)] + state_shapes
    scratch = [
        pltpu.VMEM((md, D_MODEL), f32),
        pltpu.VMEM((nseq, HIST_ROWS + tt, POOL_WIDTH), f32),
        pltpu.VMEM((mp, MLSTM_WIDTH), bf16),
        pltpu.VMEM((MLSTM_WIDTH, mp), f32),
        pltpu.VMEM((mp, 2 * MLSTM_WIDTH), bf16),
        pltpu.VMEM((md, MLSTM_WIDTH), f32),
        pltpu.VMEM((md, D_MODEL), bf16),
        pltpu.VMEM((SUBLANES, mp), f32),
        pltpu.VMEM((mp, SUBLANES), f32),
        pltpu.VMEM((HIST_ROWS, POOL_WIDTH), f32),
        pltpu.VMEM((MLSTM_HEADS, HEAD_DIM, 2 * HEAD_DIM), f32),
        pltpu.VMEM((SUBLANES, LANES), f32),
    ]
    if fuse_ffn:
        scratch += [pltpu.VMEM((2, md, D_MODEL), f32),
                    pltpu.VMEM((md, D_FF), bf16)]
    out, pool_o, c_o, n_o, m_o = pl.pallas_call(
        kern,
        grid=(n_tiles + (2 if fuse_ffn else 0),),
        in_specs=in_specs,
        out_specs=out_specs,
        out_shape=out_shape,
        scratch_shapes=scratch,
        compiler_params=pltpu.CompilerParams(
            dimension_semantics=("arbitrary",), vmem_limit_bytes=VMEM_LIMIT_BYTES),
        name="layer" if fuse_ffn else "mixer",
    )(x, *state_in, *weights)
    return out, (pool_o, c_o, n_o.reshape(batch, MLSTM_HEADS, HEAD_DIM), m_o.reshape(batch, MLSTM_HEADS))


def _ffn_kernel(h_ref, wg_ref, wu_ref, wd_ref, g_ref, b_ref, y_ref, act_ref, *, alpha):
    h = h_ref[...]
    for _ in _ffn_up_chunks(h.astype(jnp.bfloat16), wg_ref, wu_ref, act_ref):
        pass
    y_ref[...] = _layer_norm(alpha * h + _dot(act_ref[...], wd_ref[...]), g_ref[...], b_ref[...])


def _ffn(h, wg, wu, wd, g, b, *, alpha):
    n = h.shape[0]
    tm = min(n, TILE_ROWS)
    assert n % tm == 0
    row_spec = pl.BlockSpec((tm, D_MODEL), lambda i: (i, 0))
    return pl.pallas_call(
        functools.partial(_ffn_kernel, alpha=alpha),
        grid=(n // tm,),
        in_specs=[row_spec] + [_const_spec(w.shape) for w in (wg, wu, wd, g, b)],
        out_specs=row_spec,
        out_shape=jax.ShapeDtypeStruct((n, D_MODEL), jnp.float32),
        scratch_shapes=[pltpu.VMEM((tm, D_FF), jnp.bfloat16)],
        compiler_params=pltpu.CompilerParams(
            dimension_semantics=("arbitrary",), vmem_limit_bytes=VMEM_LIMIT_BYTES),
        name="ffn",
    )(h, wg, wu, wd, g, b)


def _mixer_weights(ln_in_g, ln_in_b, w_in, b_in, w_pool, pool_scale, mlstm_norm_g, w_out, ln1_g, ln1_b):
    f32, bf16 = jnp.float32, jnp.bfloat16
    k0, k1, g0 = 2 * 512, 3 * 512, 5 * 512
    w_uqvo = jnp.concatenate([w_in[:, :k0], w_in[:, k1:g0]], axis=1).astype(bf16)
    b_uqvo = jnp.concatenate([b_in[:k0], b_in[k1:g0]]).reshape(1, -1).astype(f32)
    return (
        ln_in_g.reshape(1, -1), ln_in_b.reshape(1, -1), w_uqvo, b_uqvo,
        w_in[:, k0:k1].T.astype(bf16), b_in[k0:k1].reshape(-1, 1).astype(f32),
        w_in[:, g0:].T.astype(bf16), b_in[g0:].reshape(-1, 1).astype(f32),
        w_pool.astype(bf16), pool_scale.reshape(1, -1), mlstm_norm_g.reshape(1, -1), w_out.astype(bf16),
        ln1_g.reshape(1, -1), ln1_b.reshape(1, -1),
    )


def kernel(x_prompt, x_sample, state_pool, state_mlstm_C, state_mlstm_n, state_mlstm_m, ln_in_g, ln_in_b, w_in, b_in, w_pool, pool_scale, mlstm_norm_g, w_out, ln1_g, ln1_b, w_gate, w_up, w_down, ln2_g, ln2_b):
    depth = w_in.shape[0]
    alpha = (2.0 * depth) ** 0.25
    bf16 = jnp.bfloat16
    hp, hs = x_prompt, x_sample
    acc_p, acc_s = [], []
    for l in range(depth):
        wts = _mixer_weights(ln_in_g, ln_in_b, w_in[l], b_in[l], w_pool[l], pool_scale[l], mlstm_norm_g[l],
                             w_out[l], ln1_g[l], ln1_b[l])
        ffn_w = (w_gate[l].astype(bf16), w_up[l].astype(bf16), w_down[l].astype(bf16),
                 ln2_g[l].reshape(1, -1), ln2_b[l].reshape(1, -1))
        hp, st_p = _layer(hp, None, wts, ffn_w, pos0=0, in_ln=(l == 0), alpha=alpha)
        acc_p.append(st_p)
        sample_state = (state_pool[l], state_mlstm_C[l], state_mlstm_n[l], state_mlstm_m[l])
        h1, st_s = _layer(hs, sample_state, wts, None, pos0=PAST_LEN, in_ln=(l == 0), alpha=alpha)
        hs = _ffn(h1.reshape(-1, D_MODEL), *ffn_w, alpha=alpha).reshape(hs.shape)
        acc_s.append(st_s)
    pool_p, c_p, n_p, m_p = [jnp.stack(a) for a in zip(*acc_p)]
    pool_s, c_s, n_s, m_s = [jnp.stack(a) for a in zip(*acc_s)]
    return (hp, hs, pool_p, c_p, n_p, m_p, pool_s, c_s, n_s, m_s)
```

```python
import functools

import jax
import jax.numpy as jnp
from jax import lax
from jax.experimental import pallas as pl
from jax.experimental.pallas import tpu as pltpu

D_MODEL = 1024
POOL_WIDTH = 512
POOL_WINDOWS = (2, 4, 8, 16)
POOL_GROUP_WIDTH = 128
POOL_HIST = 15
HIST_ROWS = 16
MLSTM_WIDTH = 512
MLSTM_HEADS = 4
HEAD_DIM = 128
VAUG = 2 * HEAD_DIM
D_FF = 2816
FF_CHUNK = 256
LN_EPS = 1e-5
PAST_LEN = 1024

LANES = 128
SUBLANES = 8
TILE_ROWS = 512
VMEM_LIMIT_BYTES = 60 * 1024 * 1024

_NT = (((1,), (1,)), ((), ()))


def _layer_norm(x, g, b):
    mu = jnp.mean(x, axis=-1, keepdims=True)
    xc = x - mu
    var = jnp.mean(xc * xc, axis=-1, keepdims=True)
    return xc * lax.rsqrt(var + LN_EPS) * g + b


def _log_sigmoid(x):
    return -(jnp.maximum(-x, 0.0) + jnp.log1p(jnp.exp(-jnp.abs(x))))


def _dot(a, b):
    return jnp.dot(a, b, preferred_element_type=jnp.float32)


N_FF_CHUNKS = D_FF // FF_CHUNK
LN_ROW_PARTS = 2


def _ffn_up_chunks(hb, wg_ref, wu_ref, act_ref):
    for ci in range(N_FF_CHUNKS):
        cs = slice(ci * FF_CHUNK, (ci + 1) * FF_CHUNK)
        act_ref[:, cs] = (jax.nn.silu(_dot(hb, wg_ref[:, cs])) * _dot(hb, wu_ref[:, cs])).astype(jnp.bfloat16)
        yield


def _mixer_pieces(res, x, t, hist, c_prev, m_prev, wts, scr, *, nseq, tt, pos0, in_ln, alpha):
    (lng_ref, lnb_ref, w_in_ref, b_in_ref, wkT_ref, bkT_ref, wgT_ref, bgT_ref, w_pool_ref, pscale_ref, ng_ref,
     w_out_ref, ln1g_ref, ln1b_ref) = wts
    hp_ref, ext_ref, q_ref, kT_ref, v_ref, o_ref, mix_ref, c_ref, pt_ref = scr
    f32 = jnp.float32
    bf16 = jnp.bfloat16
    md = nseq * tt
    pp = -(-tt // LANES) * LANES
    mp = nseq * pp
    cps = pp // LANES

    lane = lax.broadcasted_iota(jnp.int32, (SUBLANES, LANES), 1)
    sub = lax.broadcasted_iota(jnp.int32, (SUBLANES, LANES), 0)
    n_chunks = mp // LANES

    def gate_scans(g8, j):
        b8 = _log_sigmoid(g8)
        k = 1
        while k < LANES:
            b8 = b8 + jnp.where(lane >= k, pltpu.roll(b8, k, 1), 0.0)
            k *= 2
        c8 = g8 - pltpu.roll(b8, MLSTM_HEADS, 0)
        cm8 = c8
        k = 1
        while k < LANES:
            cm8 = jnp.maximum(cm8, jnp.where(lane >= k, pltpu.roll(cm8, k, 1), -jnp.inf))
            k *= 2
        c_ref[:, j * LANES:(j + 1) * LANES] = c8
        pt_ref[j * LANES:(j + 1) * LANES, :] = jnp.where(sub < MLSTM_HEADS, cm8, b8).T

    if pp == tt and md % (LN_ROW_PARTS * LANES) == 0:
        qr = md // LN_ROW_PARTS
    elif pp > tt:
        qr = tt
    else:
        qr = md
    pc = qr if pp == tt else pp
    hp_parts = []
    for r in range(md // qr):
        hp = _layer_norm(x[r * qr:(r + 1) * qr], lng_ref[...], lnb_ref[...]) if in_ln else x[r * qr:(r + 1) * qr]
        hp_ref[r * qr:(r + 1) * qr, :] = hp
        hb = hp.astype(bf16)
        hp_parts.append(hb)
        if pc > qr:
            hb = jnp.concatenate([hb, jnp.zeros((pc - qr, D_MODEL), bf16)], axis=0)
        cols = slice(r * pc, (r + 1) * pc)
        gT = lax.dot_general(wgT_ref[...], hb, _NT, preferred_element_type=f32) + bgT_ref[...]
        for jj in range(pc // LANES):
            gate_scans(gT[:, jj * LANES:(jj + 1) * LANES], r * (pc // LANES) + jj)
        kT = lax.dot_general(wkT_ref[...], hb, _NT, preferred_element_type=f32) + bkT_ref[...]
        kT_ref[:, cols] = kT * (HEAD_DIM ** -0.5)
        yield "ln_part"
    hp_bf = hp_parts[0] if len(hp_parts) == 1 else jnp.concatenate(hp_parts, axis=0)
    if pp > tt:
        zpad = jnp.zeros((pp - tt, D_MODEL), bf16)
        hp_bf = jnp.concatenate([a for s in range(nseq) for a in (hp_bf[s * tt:(s + 1) * tt], zpad)], axis=0)

    def dense_rows(a):
        if pp == tt:
            return a
        return jnp.concatenate([a[s * pp:s * pp + tt] for s in range(nseq)], axis=0)

    row = lax.broadcasted_iota(jnp.int32, (tt, 1), 0)
    n_avail = pos0 + t * tt + row + 1

    def pool_group(g):
        w = POOL_WINDOWS[g]
        gs = slice(g * POOL_GROUP_WIDTH, (g + 1) * POOL_GROUP_WIDTH)
        inv_cnt = 1.0 / jnp.minimum(n_avail, w).astype(f32)
        ds = []
        for s in range(nseq):
            tok = ext_ref[s, HIST_ROWS:HIST_ROWS + tt, gs]
            acc = tok
            for sh in range(1, w):
                acc = acc + ext_ref[s, HIST_ROWS - sh:HIST_ROWS - sh + tt, gs]
            ds.append(acc * inv_cnt - tok)
        d = ds[0] if nseq == 1 else jnp.concatenate(ds, axis=0)
        p = _dot(d.astype(bf16), w_pool_ref[g]) * pscale_ref[:, gs]
        mix_ref[:, gs] = p.astype(bf16)

    u = _dot(hp_bf, w_in_ref[:, 0:512]) + b_in_ref[:, 0:512]
    hist_new = []
    for s in range(nseq):
        ext_ref[s, 0:HIST_ROWS, :] = hist[s]
        ext_ref[s, HIST_ROWS:HIST_ROWS + tt, :] = u[s * pp:s * pp + tt]
        hist_new.append(ext_ref[s, tt:tt + HIST_ROWS, :])
    yield "proj"
    q = _dot(hp_bf, w_in_ref[:, 512:1024]) + b_in_ref[:, 512:1024]
    q_ref[...] = q.astype(bf16)
    pool_group(0)
    pool_group(1)
    yield "proj"
    v = _dot(hp_bf, w_in_ref[:, 1024:1536]) + b_in_ref[:, 1024:1536]
    for h in range(MLSTM_HEADS):
        v_ref[:, h * VAUG:h * VAUG + HEAD_DIM] = v[:, h * HEAD_DIM:(h + 1) * HEAD_DIM].astype(bf16)
    pool_group(2)
    yield "proj"
    o = _dot(hp_bf, w_in_ref[:, 1536:2048]) + b_in_ref[:, 1536:2048]
    o_ref[...] = dense_rows(o)
    pool_group(3)
    yield "proj_done"

    tril = (lax.broadcasted_iota(jnp.int32, (LANES, LANES), 0)
            >= lax.broadcasted_iota(jnp.int32, (LANES, LANES), 1))
    lane_row = lax.broadcasted_iota(jnp.int32, (1, LANES), 1)
    m_state = [list(m) for m in m_prev]
    c_state = [list(c) for c in c_prev]
    for j in range(n_chunks):
        s, jj = divmod(j, cps)
        rows = slice(j * LANES, (j + 1) * LANES)
        valid = min(LANES, tt - jj * LANES)
        d0 = s * tt + jj * LANES
        for h in range(MLSTM_HEADS):
            hs = slice(h * HEAD_DIM, (h + 1) * HEAD_DIM)
            qh = q_ref[rows, hs]
            kTh = kT_ref[hs, rows]
            vaug = v_ref[rows, h * VAUG:(h + 1) * VAUG]
            c_row = c_ref[h:h + 1, rows]
            cmax = jnp.broadcast_to(pt_ref[rows, h:h + 1], (LANES, LANES))
            bcum = jnp.broadcast_to(pt_ref[rows, MLSTM_HEADS + h:MLSTM_HEADS + h + 1], (LANES, LANES))
            m_row = m_state[s][h]
            c_aug = c_state[s][h]

            gmax = jnp.maximum(m_row, cmax)
            dmat = jnp.where(tril, jnp.exp(c_row - gmax), 0.0)
            sc = _dot(qh, kTh.astype(bf16)) * dmat
            sv = _dot(sc.astype(bf16), vaug)
            qc = _dot(qh, c_aug.astype(bf16))
            dec = jnp.exp(m_row - gmax)
            num = dec * qc[:, :HEAD_DIM] + sv[:, :HEAD_DIM]
            den = dec * qc[:, HEAD_DIM:] + sv[:, HEAD_DIM:]
            hh = num / jnp.maximum(jnp.abs(den), jnp.exp(-(bcum + gmax)))

            mu = jnp.mean(hh, axis=-1, keepdims=True)
            hc = hh - mu
            var = jnp.mean(hc * hc, axis=-1, keepdims=True)
            hn = hc * lax.rsqrt(var + LN_EPS)
            out = hn[:valid] * ng_ref[:, hs] * jax.nn.sigmoid(o_ref[d0:d0 + valid, hs])
            mix_ref[d0:d0 + valid, POOL_WIDTH + h * HEAD_DIM:POOL_WIDTH + (h + 1) * HEAD_DIM] = out.astype(bf16)

            g_last = gmax[valid - 1:valid, :]
            m_new = bcum[valid - 1:valid, :] + g_last
            w_state = jnp.exp(m_row - g_last)
            w_row = jnp.exp(c_row - g_last)
            if valid < LANES:
                w_row = jnp.where(lane_row < valid, w_row, 0.0)
            kv = _dot((kTh * w_row).astype(bf16), vaug)
            c_state[s][h] = jnp.concatenate([w_state, w_state], axis=1) * c_aug + kv
            m_state[s][h] = m_new
            yield "head"

    half = md // 2
    if half % SUBLANES or (nseq > 1 and half % tt):
        half = md
    h1_parts = []
    for r0 in range(0, md, half):
        mixo = _dot(mix_ref[r0:r0 + half, :], w_out_ref[...])
        yield "outproj"
        h1_parts.append(_layer_norm(alpha * hp_ref[r0:r0 + half, :] + mixo, ln1g_ref[...], ln1b_ref[...]))
    res["h1"] = h1_parts
    res["hist"] = hist_new
    res["c"] = c_state
    res["m"] = m_state
    yield "ln1"


N_MIXER_WEIGHTS = 14
N_FFN_WEIGHTS = 5
N_MIXER_SCRATCH = 9


def _layer_kernel(*refs, nseq, tt, n_t, n_tiles, pos0, in_ln, alpha, fuse_ffn, zero_init):
    assert nseq == 1 or (n_t == 1 and not fuse_ffn)
    x_ref = refs[0]
    pos = 1
    if not zero_init:
        pool0_ref, c0_ref, n0_ref, m0_ref = refs[pos:pos + 4]
        pos += 4
    wts = refs[pos:pos + N_MIXER_WEIGHTS]
    pos += N_MIXER_WEIGHTS
    if fuse_ffn:
        ffn_w = refs[pos:pos + N_FFN_WEIGHTS]
        pos += N_FFN_WEIGHTS
    out_ref, pool_out_ref, c_out_ref, n_out_ref, m_out_ref = refs[pos:pos + 5]
    pos += 5
    scr = refs[pos:pos + N_MIXER_SCRATCH]
    pos += N_MIXER_SCRATCH
    hist_st_ref, c_st_ref, m_st_ref = refs[pos:pos + 3]
    pos += 3
    if fuse_ffn:
        h1_ref, act_ref = refs[pos:pos + 2]

    f32 = jnp.float32
    mp = nseq * (-(-tt // LANES) * LANES)
    i = pl.program_id(0)
    live = i < n_tiles
    t = lax.rem(jnp.minimum(i, n_tiles - 1), jnp.int32(n_t))
    first = t == 0
    v_ref = scr[4]

    @pl.when(i == 0)
    def _init():
        for h in range(MLSTM_HEADS):
            v_ref[:, h * VAUG + HEAD_DIM:(h + 1) * VAUG] = jnp.ones((mp, HEAD_DIM), jnp.bfloat16)
        hist_st_ref[...] = jnp.zeros_like(hist_st_ref)
        c_st_ref[...] = jnp.zeros_like(c_st_ref)
        m_st_ref[...] = jnp.zeros_like(m_st_ref)
        if fuse_ffn:
            h1_ref[...] = jnp.zeros_like(h1_ref)
            act_ref[...] = jnp.zeros_like(act_ref)

    hist, c_prev, m_prev = [], [], []
    for s in range(nseq):
        if zero_init:
            hist.append(jnp.zeros((HIST_ROWS, POOL_WIDTH), f32))
            c_prev.append([jnp.zeros((HEAD_DIM, 2 * HEAD_DIM), f32)] * MLSTM_HEADS)
            m_prev.append([jnp.zeros((1, LANES), f32)] * MLSTM_HEADS)
        else:
            hist.append(jnp.concatenate(
                [jnp.zeros((HIST_ROWS - POOL_HIST, POOL_WIDTH), f32), pool0_ref[s]], axis=0))
            c_prev.append([jnp.concatenate(
                [c0_ref[s, h], jnp.broadcast_to(n0_ref[s, h], (HEAD_DIM, HEAD_DIM))], axis=1)
                for h in range(MLSTM_HEADS)])
            m_prev.append([jnp.broadcast_to(m0_ref[s, h], (1, LANES)) for h in range(MLSTM_HEADS)])
    if n_t > 1:
        hist = [jnp.where(first, hist[0], hist_st_ref[...])]
        c_prev = [[jnp.where(first, c_prev[0][h], c_st_ref[h]) for h in range(MLSTM_HEADS)]]
        m_prev = [[jnp.where(first, m_prev[0][h], m_st_ref[h:h + 1, :]) for h in range(MLSTM_HEADS)]]

    res = {}
    x = x_ref[...].reshape(nseq * tt, D_MODEL)
    mixer = _mixer_pieces(res, x, t, hist, c_prev, m_prev, wts, scr, nseq=nseq, tt=tt, pos0=pos0,
                          in_ln=in_ln, alpha=alpha)
    if fuse_ffn:
        wg_ref, wu_ref, wd_ref, ln2g_ref, ln2b_ref = ffn_w
        slot = lax.rem(i, jnp.int32(2))
        dcols = D_MODEL // LN_ROW_PARTS
        down = []

        def down_upto(n):
            while len(down) < n:
                c0 = len(down) * dcols
                down.append(_dot(act_ref[...], wd_ref[:, c0:c0 + dcols]))

        down_upto(1)
        n_heads = (mp // LANES) * MLSTM_HEADS
        body = None
        done = 0

        def body_upto(n):
            nonlocal done
            while done < min(n, N_FF_CHUNKS):
                next(body)
                done += 1

        n_proj = 0
        head = 0
        for label in mixer:
            if label == "ln_part":
                down_upto(min(len(down) + 1, LN_ROW_PARTS))
            elif label == "proj":
                n_proj += 1
                if n_proj == 1:
                    down_upto(LN_ROW_PARTS)
                    ffn_out = jnp.concatenate(down, axis=1)
                    out_ref[0] = _layer_norm(alpha * h1_ref[slot] + ffn_out, ln2g_ref[...], ln2b_ref[...])
            elif label == "proj_done":
                body = _ffn_up_chunks(h1_ref[1 - slot].astype(jnp.bfloat16), wg_ref, wu_ref, act_ref)
            elif label == "head":
                head += 1
                body_upto(-(-(N_FF_CHUNKS - 6) * head // n_heads))
            elif label == "outproj":
                body_upto(done + 3)
        assert n_proj == 3 and done == N_FF_CHUNKS
    else:
        for _ in mixer:
            pass
    h1, hist_new, c_new, m_new = res["h1"], res["hist"], res["c"], res["m"]
    r0 = 0
    for part in h1:
        if fuse_ffn:
            h1_ref[slot, r0:r0 + part.shape[0], :] = part
        elif nseq == 1:
            out_ref[0, r0:r0 + part.shape[0], :] = part
        else:
            out_ref[r0 // tt:(r0 + part.shape[0]) // tt] = part.reshape(part.shape[0] // tt, tt, D_MODEL)
        r0 += part.shape[0]
    if fuse_ffn:
        hist_new = [jnp.where(live, a, b) for a, b in zip(hist_new, hist)]
        c_new = [[jnp.where(live, a, b) for a, b in zip(cn, cp)] for cn, cp in zip(c_new, c_prev)]
        m_new = [[jnp.where(live, a, b) for a, b in zip(mn, mo)] for mn, mo in zip(m_new, m_prev)]
    if n_t > 1:
        hist_st_ref[...] = hist_new[0]
        for h in range(MLSTM_HEADS):
            c_st_ref[h] = c_new[0][h]
            m_st_ref[h:h + 1, :] = m_new[0][h]
    for s in range(nseq):
        pool_out_ref[s] = hist_new[s][HIST_ROWS - POOL_HIST:, :]
        for h in range(MLSTM_HEADS):
            c_out_ref[s, h] = c_new[s][h][:, :HEAD_DIM]
            n_out_ref[s, h] = c_new[s][h][:, HEAD_DIM:HEAD_DIM + 1]
            m_out_ref[s, h] = m_new[s][h][:, 0:1]


def _const_spec(shape):
    zeros = (0,) * len(shape)
    return pl.BlockSpec(shape, lambda *_: zeros, pipeline_mode=pl.Buffered(1))


def _layer_tiles(batch, seq_len):
    if seq_len >= TILE_ROWS:
        assert seq_len % TILE_ROWS == 0
        return 1, TILE_ROWS
    assert seq_len >= HIST_ROWS and seq_len % SUBLANES == 0
    nseq = max(1, TILE_ROWS // (-(-seq_len // LANES) * LANES))
    while batch % nseq:
        nseq -= 1
    return nseq, seq_len


def _layer(x, state, wts, ffn_w, *, pos0, in_ln, alpha):
    batch, seq_len, _ = x.shape
    nseq, tt = _layer_tiles(batch, seq_len)
    n_t = seq_len // tt
    n_tiles = (batch // nseq) * n_t
    md = nseq * tt
    mp = nseq * (-(-tt // LANES) * LANES)
    fuse_ffn = ffn_w is not None
    zero_init = state is None
    f32, bf16 = jnp.float32, jnp.bfloat16
    kern = functools.partial(_layer_kernel, nseq=nseq, tt=tt, n_t=n_t, n_tiles=n_tiles, pos0=pos0, in_ln=in_ln,
                             alpha=alpha, fuse_ffn=fuse_ffn, zero_init=zero_init)

    def mix_tile(i):
        return jnp.minimum(i, n_tiles - 1)

    def out_tile(i):
        return jnp.maximum(i - 2, 0) if fuse_ffn else i

    row3 = lambda i: (mix_tile(i) // n_t, 0, 0)
    row4 = lambda i: (mix_tile(i) // n_t, 0, 0, 0)
    state_specs = [
        pl.BlockSpec((nseq, POOL_HIST, POOL_WIDTH), row3),
        pl.BlockSpec((nseq, MLSTM_HEADS, HEAD_DIM, HEAD_DIM), row4),
        pl.BlockSpec((nseq, MLSTM_HEADS, HEAD_DIM, 1), row4),
        pl.BlockSpec((nseq, MLSTM_HEADS, 1, 1), row4),
    ]
    state_shapes = [
        jax.ShapeDtypeStruct((batch, POOL_HIST, POOL_WIDTH), f32),
        jax.ShapeDtypeStruct((batch, MLSTM_HEADS, HEAD_DIM, HEAD_DIM), f32),
        jax.ShapeDtypeStruct((batch, MLSTM_HEADS, HEAD_DIM, 1), f32),
        jax.ShapeDtypeStruct((batch, MLSTM_HEADS, 1, 1), f32),
    ]
    state_in = ()
    if not zero_init:
        pool, c, n, m = state
        state_in = (pool, c, n.reshape(batch, MLSTM_HEADS, HEAD_DIM, 1), m.reshape(batch, MLSTM_HEADS, 1, 1))
    weights = wts + (ffn_w if fuse_ffn else ())
    in_specs = ([pl.BlockSpec((nseq, tt, D_MODEL), lambda i: (mix_tile(i) // n_t, mix_tile(i) % n_t, 0))]
                + (state_specs if not zero_init else []) + [_const_spec(w.shape) for w in weights])
    out_specs = [pl.BlockSpec((nseq, tt, D_MODEL), lambda i: (out_tile(i) // n_t, out_tile(i) % n_t, 0))
                 ] + state_specs
    out_shape = [jax.ShapeDtypeStruct((batch, seq_len, D_MODEL), f32)] + state_shapes
    scratch = [
        pltpu.VMEM((md, D_MODEL), f32),
        pltpu.VMEM((nseq, HIST_ROWS + tt, POOL_WIDTH), f32),
        pltpu.VMEM((mp, MLSTM_WIDTH), bf16),
        pltpu.VMEM((MLSTM_WIDTH, mp), f32),
        pltpu.VMEM((mp, 2 * MLSTM_WIDTH), bf16),
        pltpu.VMEM((md, MLSTM_WIDTH), f32),
        pltpu.VMEM((md, D_MODEL), bf16),
        pltpu.VMEM((SUBLANES, mp), f32),
        pltpu.VMEM((mp, SUBLANES), f32),
        pltpu.VMEM((HIST_ROWS, POOL_WIDTH), f32),
        pltpu.VMEM((MLSTM_HEADS, HEAD_DIM, 2 * HEAD_DIM), f32),
        pltpu.VMEM((SUBLANES, LANES), f32),
    ]
    if fuse_ffn:
        scratch += [pltpu.VMEM((2, md, D_MODEL), f32),
                    pltpu.VMEM((md, D_FF), bf16)]
    out, pool_o, c_o, n_o, m_o = pl.pallas_call(
        kern,
        grid=(n_tiles + (2 if fuse_ffn else 0),),
        in_specs=in_specs,
        out_specs=out_specs,
        out_shape=out_shape,
        scratch_shapes=scratch,
        compiler_params=pltpu.CompilerParams(
            dimension_semantics=("arbitrary",), vmem_limit_bytes=VMEM_LIMIT_BYTES),
        name="layer" if fuse_ffn else "mixer",
    )(x, *state_in, *weights)
    return out, (pool_o, c_o, n_o.reshape(batch, MLSTM_HEADS, HEAD_DIM), m_o.reshape(batch, MLSTM_HEADS))


def _ffn_kernel(h_ref, wg_ref, wu_ref, wd_ref, g_ref, b_ref, y_ref, act_ref, *, alpha):
    h = h_ref[...]
    for _ in _ffn_up_chunks(h.astype(jnp.bfloat16), wg_ref, wu_ref, act_ref):
        pass
    y_ref[...] = _layer_norm(alpha * h + _dot(act_ref[...], wd_ref[...]), g_ref[...], b_ref[...])


def _ffn(h, wg, wu, wd, g, b, *, alpha):
    n = h.shape[0]
    tm = min(n, TILE_ROWS)
    assert n % tm == 0
    row_spec = pl.BlockSpec((tm, D_MODEL), lambda i: (i, 0))
    return pl.pallas_call(
        functools.partial(_ffn_kernel, alpha=alpha),
        grid=(n // tm,),
        in_specs=[row_spec] + [_const_spec(w.shape) for w in (wg, wu, wd, g, b)],
        out_specs=row_spec,
        out_shape=jax.ShapeDtypeStruct((n, D_MODEL), jnp.float32),
        scratch_shapes=[pltpu.VMEM((tm, D_FF), jnp.bfloat16)],
        compiler_params=pltpu.CompilerParams(
            dimension_semantics=("arbitrary",), vmem_limit_bytes=VMEM_LIMIT_BYTES),
        name="ffn",
    )(h, wg, wu, wd, g, b)


def _mixer_weights(ln_in_g, ln_in_b, w_in, b_in, w_pool, pool_scale, mlstm_norm_g, w_out, ln1_g, ln1_b):
    f32, bf16 = jnp.float32, jnp.bfloat16
    k0, k1, g0 = 2 * 512, 3 * 512, 5 * 512
    w_uqvo = jnp.concatenate([w_in[:, :k0], w_in[:, k1:g0]], axis=1).astype(bf16)
    b_uqvo = jnp.concatenate([b_in[:k0], b_in[k1:g0]]).reshape(1, -1).astype(f32)
    return (
        ln_in_g.reshape(1, -1), ln_in_b.reshape(1, -1), w_uqvo, b_uqvo,
        w_in[:, k0:k1].T.astype(bf16), b_in[k0:k1].reshape(-1, 1).astype(f32),
        w_in[:, g0:].T.astype(bf16), b_in[g0:].reshape(-1, 1).astype(f32),
        w_pool.astype(bf16), pool_scale.reshape(1, -1), mlstm_norm_g.reshape(1, -1), w_out.astype(bf16),
        ln1_g.reshape(1, -1), ln1_b.reshape(1, -1),
    )


def kernel(x_prompt, x_sample, state_pool, state_mlstm_C, state_mlstm_n, state_mlstm_m, ln_in_g, ln_in_b, w_in, b_in, w_pool, pool_scale, mlstm_norm_g, w_out, ln1_g, ln1_b, w_gate, w_up, w_down, ln2_g, ln2_b):
    depth = w_in.shape[0]
    alpha = (2.0 * depth) ** 0.25
    bf16 = jnp.bfloat16
    hp, hs = x_prompt, x_sample
    acc_p, acc_s = [], []
    for l in range(depth):
        wts = _mixer_weights(ln_in_g, ln_in_b, w_in[l], b_in[l], w_pool[l], pool_scale[l], mlstm_norm_g[l],
                             w_out[l], ln1_g[l], ln1_b[l])
        ffn_w = (w_gate[l].astype(bf16), w_up[l].astype(bf16), w_down[l].astype(bf16),
                 ln2_g[l].reshape(1, -1), ln2_b[l].reshape(1, -1))
        hp, st_p = _layer(hp, None, wts, ffn_w, pos0=0, in_ln=(l == 0), alpha=alpha)
        acc_p.append(st_p)
        sample_state = (state_pool[l], state_mlstm_C[l], state_mlstm_n[l], state_mlstm_m[l])
        h1, st_s = _layer(hs, sample_state, wts, None, pos0=PAST_LEN, in_ln=(l == 0), alpha=alpha)
        hs = _ffn(h1.reshape(-1, D_MODEL), *ffn_w, alpha=alpha).reshape(hs.shape)
        acc_s.append(st_s)
    pool_p, c_p, n_p, m_p = [jnp.stack(a) for a in zip(*acc_p)]
    pool_s, c_s, n_s, m_s = [jnp.stack(a) for a in zip(*acc_s)]
    return (hp, hs, pool_p, c_p, n_p, m_p, pool_s, c_s, n_s, m_s)
```

```python
import functools

import jax
import jax.numpy as jnp
from jax import lax
from jax.experimental import pallas as pl
from jax.experimental.pallas import tpu as pltpu

D_MODEL = 1024
POOL_WIDTH = 512
POOL_WINDOWS = (2, 4, 8, 16)
POOL_GROUP_WIDTH = 128
POOL_HIST = 15
HIST_ROWS = 16
MLSTM_WIDTH = 512
MLSTM_HEADS = 4
HEAD_DIM = 128
VAUG = 2 * HEAD_DIM
D_FF = 2816
FF_CHUNK = 256
LN_EPS = 1e-5
PAST_LEN = 1024

LANES = 128
SUBLANES = 8
TILE_ROWS = 512
SHORT_TILE_ROWS = 1024
FFN_TILE_ROWS = 1024
VMEM_LIMIT_BYTES = 60 * 1024 * 1024

_NT = (((1,), (1,)), ((), ()))


def _layer_norm(x, g, b):
    mu = jnp.mean(x, axis=-1, keepdims=True)
    xc = x - mu
    var = jnp.mean(xc * xc, axis=-1, keepdims=True)
    return xc * lax.rsqrt(var + LN_EPS) * g + b


def _log_sigmoid(x):
    return -(jnp.maximum(-x, 0.0) + jnp.log1p(jnp.exp(-jnp.abs(x))))


def _dot(a, b):
    return jnp.dot(a, b, preferred_element_type=jnp.float32)


N_FF_CHUNKS = D_FF // FF_CHUNK
LN_ROW_PARTS = 2


def _ffn_up_chunks(hb, wg_ref, wu_ref, act_ref):
    for ci in range(N_FF_CHUNKS):
        cs = slice(ci * FF_CHUNK, (ci + 1) * FF_CHUNK)
        act_ref[:, cs] = (jax.nn.silu(_dot(hb, wg_ref[:, cs])) * _dot(hb, wu_ref[:, cs])).astype(jnp.bfloat16)
        yield


def _mixer_pieces(res, x, t, hist, c_prev, m_prev, wts, scr, *, nseq, tt, pos0, in_ln, alpha):
    (lng_ref, lnb_ref, w_in_ref, b_in_ref, wkT_ref, bkT_ref, wgT_ref, bgT_ref, w_pool_ref, pscale_ref, ng_ref,
     w_out_ref, ln1g_ref, ln1b_ref) = wts
    hp_ref, ext_ref, q_ref, kT_ref, v_ref, o_ref, mix_ref, c_ref, pt_ref = scr
    f32 = jnp.float32
    bf16 = jnp.bfloat16
    md = nseq * tt
    pp = -(-tt // LANES) * LANES
    mp = nseq * pp
    cps = pp // LANES

    lane = lax.broadcasted_iota(jnp.int32, (SUBLANES, LANES), 1)
    sub = lax.broadcasted_iota(jnp.int32, (SUBLANES, LANES), 0)
    n_chunks = mp // LANES

    def gate_scans(g8, j):
        b8 = _log_sigmoid(g8)
        k = 1
        while k < LANES:
            b8 = b8 + jnp.where(lane >= k, pltpu.roll(b8, k, 1), 0.0)
            k *= 2
        c8 = g8 - pltpu.roll(b8, MLSTM_HEADS, 0)
        cm8 = c8
        k = 1
        while k < LANES:
            cm8 = jnp.maximum(cm8, jnp.where(lane >= k, pltpu.roll(cm8, k, 1), -jnp.inf))
            k *= 2
        c_ref[:, j * LANES:(j + 1) * LANES] = c8
        pt_ref[j * LANES:(j + 1) * LANES, :] = jnp.where(sub < MLSTM_HEADS, cm8, b8).T

    if pp == tt and md % (LN_ROW_PARTS * LANES) == 0:
        qr = md // LN_ROW_PARTS
    elif pp > tt:
        qr = tt
    else:
        qr = md
    pc = qr if pp == tt else pp
    hp_parts = []
    for r in range(md // qr):
        hp = _layer_norm(x[r * qr:(r + 1) * qr], lng_ref[...], lnb_ref[...]) if in_ln else x[r * qr:(r + 1) * qr]
        hp_ref[r * qr:(r + 1) * qr, :] = hp
        hb = hp.astype(bf16)
        hp_parts.append(hb)
        if pc > qr:
            hb = jnp.concatenate([hb, jnp.zeros((pc - qr, D_MODEL), bf16)], axis=0)
        cols = slice(r * pc, (r + 1) * pc)
        gT = lax.dot_general(wgT_ref[...], hb, _NT, preferred_element_type=f32) + bgT_ref[...]
        for jj in range(pc // LANES):
            gate_scans(gT[:, jj * LANES:(jj + 1) * LANES], r * (pc // LANES) + jj)
        kT = lax.dot_general(wkT_ref[...], hb, _NT, preferred_element_type=f32) + bkT_ref[...]
        kT_ref[:, cols] = kT * (HEAD_DIM ** -0.5)
        yield "ln_part"
    hp_bf = hp_parts[0] if len(hp_parts) == 1 else jnp.concatenate(hp_parts, axis=0)

    def store_padded(ref, lanes, a):
        for s in range(nseq):
            ref[s * pp:s * pp + tt, lanes] = a[s * tt:(s + 1) * tt]

    row = lax.broadcasted_iota(jnp.int32, (tt, 1), 0)
    n_avail = pos0 + t * tt + row + 1

    def pool_group(g):
        w = POOL_WINDOWS[g]
        gs = slice(g * POOL_GROUP_WIDTH, (g + 1) * POOL_GROUP_WIDTH)
        inv_cnt = 1.0 / jnp.minimum(n_avail, w).astype(f32)
        ds = []
        for s in range(nseq):
            tok = ext_ref[s, HIST_ROWS:HIST_ROWS + tt, gs]
            acc = tok
            for sh in range(1, w):
                acc = acc + ext_ref[s, HIST_ROWS - sh:HIST_ROWS - sh + tt, gs]
            ds.append(acc * inv_cnt - tok)
        d = ds[0] if nseq == 1 else jnp.concatenate(ds, axis=0)
        p = _dot(d.astype(bf16), w_pool_ref[g]) * pscale_ref[:, gs]
        mix_ref[:, gs] = p.astype(bf16)

    u = _dot(hp_bf, w_in_ref[:, 0:512]) + b_in_ref[:, 0:512]
    hist_new = []
    for s in range(nseq):
        ext_ref[s, 0:HIST_ROWS, :] = hist[s]
        ext_ref[s, HIST_ROWS:HIST_ROWS + tt, :] = u[s * tt:(s + 1) * tt]
        hist_new.append(ext_ref[s, tt:tt + HIST_ROWS, :])
    yield "proj"
    q = _dot(hp_bf, w_in_ref[:, 512:1024]) + b_in_ref[:, 512:1024]
    store_padded(q_ref, slice(None), q.astype(bf16))
    pool_group(0)
    pool_group(1)
    yield "proj"
    v = _dot(hp_bf, w_in_ref[:, 1024:1536]) + b_in_ref[:, 1024:1536]
    for h in range(MLSTM_HEADS):
        store_padded(v_ref, slice(h * VAUG, h * VAUG + HEAD_DIM), v[:, h * HEAD_DIM:(h + 1) * HEAD_DIM].astype(bf16))
    pool_group(2)
    yield "proj"
    o = _dot(hp_bf, w_in_ref[:, 1536:2048]) + b_in_ref[:, 1536:2048]
    o_ref[...] = o
    pool_group(3)
    yield "proj_done"

    tril = (lax.broadcasted_iota(jnp.int32, (LANES, LANES), 0)
            >= lax.broadcasted_iota(jnp.int32, (LANES, LANES), 1))
    lane_row = lax.broadcasted_iota(jnp.int32, (1, LANES), 1)
    m_state = [list(m) for m in m_prev]
    c_state = [list(c) for c in c_prev]
    for j in range(n_chunks):
        s, jj = divmod(j, cps)
        rows = slice(j * LANES, (j + 1) * LANES)
        valid = min(LANES, tt - jj * LANES)
        d0 = s * tt + jj * LANES
        for h in range(MLSTM_HEADS):
            hs = slice(h * HEAD_DIM, (h + 1) * HEAD_DIM)
            qh = q_ref[rows, hs]
            kTh = kT_ref[hs, rows]
            vaug = v_ref[rows, h * VAUG:(h + 1) * VAUG]
            c_row = c_ref[h:h + 1, rows]
            cmax = jnp.broadcast_to(pt_ref[rows, h:h + 1], (LANES, LANES))
            bcum = jnp.broadcast_to(pt_ref[rows, MLSTM_HEADS + h:MLSTM_HEADS + h + 1], (LANES, LANES))
            m_row = m_state[s][h]
            c_aug = c_state[s][h]

            gmax = jnp.maximum(m_row, cmax)
            dmat = jnp.where(tril, jnp.exp(c_row - gmax), 0.0)
            sc = _dot(qh, kTh.astype(bf16)) * dmat
            sv = _dot(sc.astype(bf16), vaug)
            qc = _dot(qh, c_aug.astype(bf16))
            dec = jnp.exp(m_row - gmax)
            num = dec * qc[:, :HEAD_DIM] + sv[:, :HEAD_DIM]
            den = dec * qc[:, HEAD_DIM:] + sv[:, HEAD_DIM:]
            hh = num / jnp.maximum(jnp.abs(den), jnp.exp(-(bcum + gmax)))

            mu = jnp.mean(hh, axis=-1, keepdims=True)
            hc = hh - mu
            var = jnp.mean(hc * hc, axis=-1, keepdims=True)
            hn = hc * lax.rsqrt(var + LN_EPS)
            out = hn[:valid] * ng_ref[:, hs] * jax.nn.sigmoid(o_ref[d0:d0 + valid, hs])
            mix_ref[d0:d0 + valid, POOL_WIDTH + h * HEAD_DIM:POOL_WIDTH + (h + 1) * HEAD_DIM] = out.astype(bf16)

            g_last = gmax[valid - 1:valid, :]
            m_new = bcum[valid - 1:valid, :] + g_last
            w_state = jnp.exp(m_row - g_last)
            w_row = jnp.exp(c_row - g_last)
            if valid < LANES:
                w_row = jnp.where(lane_row < valid, w_row, 0.0)
            kv = _dot((kTh * w_row).astype(bf16), vaug)
            c_state[s][h] = jnp.concatenate([w_state, w_state], axis=1) * c_aug + kv
            m_state[s][h] = m_new
            yield "head"

    half = md // 2
    if half % SUBLANES or (nseq > 1 and half % tt):
        half = md
    h1_parts = []
    for r0 in range(0, md, half):
        mixo = _dot(mix_ref[r0:r0 + half, :], w_out_ref[...])
        yield "outproj"
        h1_parts.append(_layer_norm(alpha * hp_ref[r0:r0 + half, :] + mixo, ln1g_ref[...], ln1b_ref[...]))
    res["h1"] = h1_parts
    res["hist"] = hist_new
    res["c"] = c_state
    res["m"] = m_state
    yield "ln1"


N_MIXER_WEIGHTS = 14
N_FFN_WEIGHTS = 5
N_MIXER_SCRATCH = 9


def _layer_kernel(*refs, nseq, tt, n_t, n_tiles, pos0, in_ln, alpha, fuse_ffn, zero_init):
    assert nseq == 1 or (n_t == 1 and not fuse_ffn)
    x_ref = refs[0]
    pos = 1
    if not zero_init:
        pool0_ref, c0_ref, n0_ref, m0_ref = refs[pos:pos + 4]
        pos += 4
    wts = refs[pos:pos + N_MIXER_WEIGHTS]
    pos += N_MIXER_WEIGHTS
    if fuse_ffn:
        ffn_w = refs[pos:pos + N_FFN_WEIGHTS]
        pos += N_FFN_WEIGHTS
    out_ref, pool_out_ref, c_out_ref, n_out_ref, m_out_ref = refs[pos:pos + 5]
    pos += 5
    scr = refs[pos:pos + N_MIXER_SCRATCH]
    pos += N_MIXER_SCRATCH
    hist_st_ref, c_st_ref, m_st_ref = refs[pos:pos + 3]
    pos += 3
    if fuse_ffn:
        h1_ref, act_ref = refs[pos:pos + 2]

    f32 = jnp.float32
    mp = nseq * (-(-tt // LANES) * LANES)
    i = pl.program_id(0)
    live = i < n_tiles
    t = lax.rem(jnp.minimum(i, n_tiles - 1), jnp.int32(n_t))
    first = t == 0
    q_ref, v_ref = scr[2], scr[4]

    @pl.when(i == 0)
    def _init():
        q_ref[...] = jnp.zeros_like(q_ref)
        for h in range(MLSTM_HEADS):
            v_ref[:, h * VAUG:h * VAUG + HEAD_DIM] = jnp.zeros((mp, HEAD_DIM), jnp.bfloat16)
            v_ref[:, h * VAUG + HEAD_DIM:(h + 1) * VAUG] = jnp.ones((mp, HEAD_DIM), jnp.bfloat16)
        hist_st_ref[...] = jnp.zeros_like(hist_st_ref)
        c_st_ref[...] = jnp.zeros_like(c_st_ref)
        m_st_ref[...] = jnp.zeros_like(m_st_ref)
        if fuse_ffn:
            h1_ref[...] = jnp.zeros_like(h1_ref)
            act_ref[...] = jnp.zeros_like(act_ref)

    hist, c_prev, m_prev = [], [], []
    for s in range(nseq):
        if zero_init:
            hist.append(jnp.zeros((HIST_ROWS, POOL_WIDTH), f32))
            c_prev.append([jnp.zeros((HEAD_DIM, 2 * HEAD_DIM), f32)] * MLSTM_HEADS)
            m_prev.append([jnp.zeros((1, LANES), f32)] * MLSTM_HEADS)
        else:
            hist.append(jnp.concatenate(
                [jnp.zeros((HIST_ROWS - POOL_HIST, POOL_WIDTH), f32), pool0_ref[s]], axis=0))
            c_prev.append([jnp.concatenate(
                [c0_ref[s, h], jnp.broadcast_to(n0_ref[s, h], (HEAD_DIM, HEAD_DIM))], axis=1)
                for h in range(MLSTM_HEADS)])
            m_prev.append([jnp.broadcast_to(m0_ref[s, h], (1, LANES)) for h in range(MLSTM_HEADS)])
    if n_t > 1:
        hist = [jnp.where(first, hist[0], hist_st_ref[...])]
        c_prev = [[jnp.where(first, c_prev[0][h], c_st_ref[h]) for h in range(MLSTM_HEADS)]]
        m_prev = [[jnp.where(first, m_prev[0][h], m_st_ref[h:h + 1, :]) for h in range(MLSTM_HEADS)]]

    res = {}
    x = x_ref[...].reshape(nseq * tt, D_MODEL)
    mixer = _mixer_pieces(res, x, t, hist, c_prev, m_prev, wts, scr, nseq=nseq, tt=tt, pos0=pos0,
                          in_ln=in_ln, alpha=alpha)
    if fuse_ffn:
        wg_ref, wu_ref, wd_ref, ln2g_ref, ln2b_ref = ffn_w
        slot = lax.rem(i, jnp.int32(2))
        dcols = D_MODEL // LN_ROW_PARTS
        down = []

        def down_upto(n):
            while len(down) < n:
                c0 = len(down) * dcols
                down.append(_dot(act_ref[...], wd_ref[:, c0:c0 + dcols]))

        down_upto(1)
        n_heads = (mp // LANES) * MLSTM_HEADS
        body = None
        done = 0

        def body_upto(n):
            nonlocal done
            while done < min(n, N_FF_CHUNKS):
                next(body)
                done += 1

        n_proj = 0
        head = 0
        for label in mixer:
            if label == "ln_part":
                down_upto(min(len(down) + 1, LN_ROW_PARTS))
            elif label == "proj":
                n_proj += 1
                if n_proj <= 2:
                    down_upto(LN_ROW_PARTS)
                    rs = slice((n_proj - 1) * (tt // 2), n_proj * (tt // 2))
                    ffn_out = jnp.concatenate([d[rs] for d in down], axis=1)
                    out_ref[0, rs, :] = _layer_norm(alpha * h1_ref[slot, rs, :] + ffn_out, ln2g_ref[...],
                                                    ln2b_ref[...])
            elif label == "proj_done":
                body = _ffn_up_chunks(h1_ref[1 - slot].astype(jnp.bfloat16), wg_ref, wu_ref, act_ref)
            elif label == "head":
                head += 1
                body_upto(-(-(N_FF_CHUNKS - 4) * head // n_heads))
            elif label == "outproj":
                body_upto(done + 2)
        assert n_proj == 3 and done == N_FF_CHUNKS
    else:
        for _ in mixer:
            pass
    h1, hist_new, c_new, m_new = res["h1"], res["hist"], res["c"], res["m"]
    r0 = 0
    for part in h1:
        if fuse_ffn:
            h1_ref[slot, r0:r0 + part.shape[0], :] = part
        elif nseq == 1:
            out_ref[0, r0:r0 + part.shape[0], :] = part
        else:
            out_ref[r0 // tt:(r0 + part.shape[0]) // tt] = part.reshape(part.shape[0] // tt, tt, D_MODEL)
        r0 += part.shape[0]
    if fuse_ffn:
        hist_new = [jnp.where(live, a, b) for a, b in zip(hist_new, hist)]
        c_new = [[jnp.where(live, a, b) for a, b in zip(cn, cp)] for cn, cp in zip(c_new, c_prev)]
        m_new = [[jnp.where(live, a, b) for a, b in zip(mn, mo)] for mn, mo in zip(m_new, m_prev)]
    if n_t > 1:
        hist_st_ref[...] = hist_new[0]
        for h in range(MLSTM_HEADS):
            c_st_ref[h] = c_new[0][h]
            m_st_ref[h:h + 1, :] = m_new[0][h]
    for s in range(nseq):
        pool_out_ref[s] = hist_new[s][HIST_ROWS - POOL_HIST:, :]
        for h in range(MLSTM_HEADS):
            c_out_ref[s, h] = c_new[s][h][:, :HEAD_DIM]
            n_out_ref[s, h] = c_new[s][h][:, HEAD_DIM:HEAD_DIM + 1]
            m_out_ref[s, h] = m_new[s][h][:, 0:1]


def _const_spec(shape):
    zeros = (0,) * len(shape)
    return pl.BlockSpec(shape, lambda *_: zeros, pipeline_mode=pl.Buffered(1))


def _layer_tiles(batch, seq_len):
    if seq_len >= TILE_ROWS:
        assert seq_len % TILE_ROWS == 0
        return 1, TILE_ROWS
    assert seq_len >= HIST_ROWS and seq_len % SUBLANES == 0
    nseq = max(1, SHORT_TILE_ROWS // (-(-seq_len // LANES) * LANES))
    while batch % nseq:
        nseq -= 1
    return nseq, seq_len


def _layer(x, state, wts, ffn_w, *, pos0, in_ln, alpha):
    batch, seq_len, _ = x.shape
    nseq, tt = _layer_tiles(batch, seq_len)
    n_t = seq_len // tt
    n_tiles = (batch // nseq) * n_t
    md = nseq * tt
    mp = nseq * (-(-tt // LANES) * LANES)
    fuse_ffn = ffn_w is not None
    zero_init = state is None
    f32, bf16 = jnp.float32, jnp.bfloat16
    kern = functools.partial(_layer_kernel, nseq=nseq, tt=tt, n_t=n_t, n_tiles=n_tiles, pos0=pos0, in_ln=in_ln,
                             alpha=alpha, fuse_ffn=fuse_ffn, zero_init=zero_init)

    def mix_tile(i):
        return jnp.minimum(i, n_tiles - 1)

    def out_tile(i):
        return jnp.maximum(i - 2, 0) if fuse_ffn else i

    row3 = lambda i: (mix_tile(i) // n_t, 0, 0)
    row4 = lambda i: (mix_tile(i) // n_t, 0, 0, 0)
    state_specs = [
        pl.BlockSpec((nseq, POOL_HIST, POOL_WIDTH), row3),
        pl.BlockSpec((nseq, MLSTM_HEADS, HEAD_DIM, HEAD_DIM), row4),
        pl.BlockSpec((nseq, MLSTM_HEADS, HEAD_DIM, 1), row4),
        pl.BlockSpec((nseq, MLSTM_HEADS, 1, 1), row4),
    ]
    state_shapes = [
        jax.ShapeDtypeStruct((batch, POOL_HIST, POOL_WIDTH), f32),
        jax.ShapeDtypeStruct((batch, MLSTM_HEADS, HEAD_DIM, HEAD_DIM), f32),
        jax.ShapeDtypeStruct((batch, MLSTM_HEADS, HEAD_DIM, 1), f32),
        jax.ShapeDtypeStruct((batch, MLSTM_HEADS, 1, 1), f32),
    ]
    state_in = ()
    if not zero_init:
        pool, c, n, m = state
        state_in = (pool, c, n.reshape(batch, MLSTM_HEADS, HEAD_DIM, 1), m.reshape(batch, MLSTM_HEADS, 1, 1))
    weights = wts + (ffn_w if fuse_ffn else ())
    in_specs = ([pl.BlockSpec((nseq, tt, D_MODEL), lambda i: (mix_tile(i) // n_t, mix_tile(i) % n_t, 0))]
                + (state_specs if not zero_init else []) + [_const_spec(w.shape) for w in weights])
    out_specs = [pl.BlockSpec((nseq, tt, D_MODEL), lambda i: (out_tile(i) // n_t, out_tile(i) % n_t, 0))
                 ] + state_specs
    out_shape = [jax.ShapeDtypeStruct((batch, seq_len, D_MODEL), f32)] + state_shapes
    scratch = [
        pltpu.VMEM((md, D_MODEL), f32),
        pltpu.VMEM((nseq, HIST_ROWS + tt, POOL_WIDTH), f32),
        pltpu.VMEM((mp, MLSTM_WIDTH), bf16),
        pltpu.VMEM((MLSTM_WIDTH, mp), f32),
        pltpu.VMEM((mp, 2 * MLSTM_WIDTH), bf16),
        pltpu.VMEM((md, MLSTM_WIDTH), f32),
        pltpu.VMEM((md, D_MODEL), bf16),
        pltpu.VMEM((SUBLANES, mp), f32),
        pltpu.VMEM((mp, SUBLANES), f32),
        pltpu.VMEM((HIST_ROWS, POOL_WIDTH), f32),
        pltpu.VMEM((MLSTM_HEADS, HEAD_DIM, 2 * HEAD_DIM), f32),
        pltpu.VMEM((SUBLANES, LANES), f32),
    ]
    if fuse_ffn:
        scratch += [pltpu.VMEM((2, md, D_MODEL), f32),
                    pltpu.VMEM((md, D_FF), bf16)]
    out, pool_o, c_o, n_o, m_o = pl.pallas_call(
        kern,
        grid=(n_tiles + (2 if fuse_ffn else 0),),
        in_specs=in_specs,
        out_specs=out_specs,
        out_shape=out_shape,
        scratch_shapes=scratch,
        compiler_params=pltpu.CompilerParams(
            dimension_semantics=("arbitrary",), vmem_limit_bytes=VMEM_LIMIT_BYTES),
        name="layer" if fuse_ffn else "mixer",
    )(x, *state_in, *weights)
    return out, (pool_o, c_o, n_o.reshape(batch, MLSTM_HEADS, HEAD_DIM), m_o.reshape(batch, MLSTM_HEADS))


def _ffn_kernel(h_ref, wg_ref, wu_ref, wd_ref, g_ref, b_ref, y_ref, act_ref, *, alpha):
    h = h_ref[...]
    for _ in _ffn_up_chunks(h.astype(jnp.bfloat16), wg_ref, wu_ref, act_ref):
        pass
    y_ref[...] = _layer_norm(alpha * h + _dot(act_ref[...], wd_ref[...]), g_ref[...], b_ref[...])


def _ffn(h, wg, wu, wd, g, b, *, alpha):
    n = h.shape[0]
    tm = min(n, FFN_TILE_ROWS)
    assert n % tm == 0
    row_spec = pl.BlockSpec((tm, D_MODEL), lambda i: (i, 0))
    return pl.pallas_call(
        functools.partial(_ffn_kernel, alpha=alpha),
        grid=(n // tm,),
        in_specs=[row_spec] + [_const_spec(w.shape) for w in (wg, wu, wd, g, b)],
        out_specs=row_spec,
        out_shape=jax.ShapeDtypeStruct((n, D_MODEL), jnp.float32),
        scratch_shapes=[pltpu.VMEM((tm, D_FF), jnp.bfloat16)],
        compiler_params=pltpu.CompilerParams(
            dimension_semantics=("arbitrary",), vmem_limit_bytes=VMEM_LIMIT_BYTES),
        name="ffn",
    )(h, wg, wu, wd, g, b)


def _mixer_weights(ln_in_g, ln_in_b, w_in, b_in, w_pool, pool_scale, mlstm_norm_g, w_out, ln1_g, ln1_b):
    f32, bf16 = jnp.float32, jnp.bfloat16
    k0, k1, g0 = 2 * 512, 3 * 512, 5 * 512
    w_uqvo = jnp.concatenate([w_in[:, :k0], w_in[:, k1:g0]], axis=1).astype(bf16)
    b_uqvo = jnp.concatenate([b_in[:k0], b_in[k1:g0]]).reshape(1, -1).astype(f32)
    return (
        ln_in_g.reshape(1, -1), ln_in_b.reshape(1, -1), w_uqvo, b_uqvo,
        w_in[:, k0:k1].T.astype(bf16), b_in[k0:k1].reshape(-1, 1).astype(f32),
        w_in[:, g0:].T.astype(bf16), b_in[g0:].reshape(-1, 1).astype(f32),
        w_pool.astype(bf16), pool_scale.reshape(1, -1), mlstm_norm_g.reshape(1, -1), w_out.astype(bf16),
        ln1_g.reshape(1, -1), ln1_b.reshape(1, -1),
    )


def kernel(x_prompt, x_sample, state_pool, state_mlstm_C, state_mlstm_n, state_mlstm_m, ln_in_g, ln_in_b, w_in, b_in, w_pool, pool_scale, mlstm_norm_g, w_out, ln1_g, ln1_b, w_gate, w_up, w_down, ln2_g, ln2_b):
    depth = w_in.shape[0]
    alpha = (2.0 * depth) ** 0.25
    bf16 = jnp.bfloat16
    hp, hs = x_prompt, x_sample
    acc_p, acc_s = [], []
    for l in range(depth):
        wts = _mixer_weights(ln_in_g, ln_in_b, w_in[l], b_in[l], w_pool[l], pool_scale[l], mlstm_norm_g[l],
                             w_out[l], ln1_g[l], ln1_b[l])
        ffn_w = (w_gate[l].astype(bf16), w_up[l].astype(bf16), w_down[l].astype(bf16),
                 ln2_g[l].reshape(1, -1), ln2_b[l].reshape(1, -1))
        hp, st_p = _layer(hp, None, wts, ffn_w, pos0=0, in_ln=(l == 0), alpha=alpha)
        acc_p.append(st_p)
        sample_state = (state_pool[l], state_mlstm_C[l], state_mlstm_n[l], state_mlstm_m[l])
        h1, st_s = _layer(hs, sample_state, wts, None, pos0=PAST_LEN, in_ln=(l == 0), alpha=alpha)
        hs = _ffn(h1.reshape(-1, D_MODEL), *ffn_w, alpha=alpha).reshape(hs.shape)
        acc_s.append(st_s)
    pool_p, c_p, n_p, m_p = [jnp.stack(a) for a in zip(*acc_p)]
    pool_s, c_s, n_s, m_s = [jnp.stack(a) for a in zip(*acc_s)]
    return (hp, hs, pool_p, c_p, n_p, m_p, pool_s, c_s, n_s, m_s)
```
